```python
import math
import jax, jax.numpy as jnp
from jax import lax
import numpy as np

D_MODEL = 1024
BATCH = 4
SEQ = 4096
DEPTH = 1
DEC_BATCH = 32
DEC_SEQ = 1
PAST_LEN = 8192
PAGE_SIZE = 128

D_MIX = D_MODEL
D_RET = D_MIX // 2
D_DIFF = D_MIX - D_RET
H_RET = 4
DK_RET = D_RET // H_RET
DV_RET = D_RET // H_RET
H_DIFF = 4
DV_DIFF = D_DIFF // H_DIFF
DH_DIFF = DV_DIFF // 2
ROT_DIFF = DH_DIFF // 4
ROPE_THETA = 500000.0
RET_ROPE_THETA = 10000.0
RET_CHUNK = 128
Q_BLOCK = 128
EPS = 1e-6
NEG_INF = -1e30

kernel_name = "hymba_retention_diffattn_step"


def rms_norm(x, g):
    xf = x.astype(jnp.float32)
    y = xf * lax.rsqrt(jnp.mean(xf * xf, axis=-1, keepdims=True) + EPS)
    return (y * g.astype(jnp.float32)).astype(x.dtype)


def rope(x, pos, theta, n_rot):
    half = n_rot // 2
    freqs = theta ** (-jnp.arange(0, n_rot, 2, dtype=jnp.float32) / n_rot)
    ang = pos.astype(jnp.float32)[:, None] * freqs[None, :]
    ang = ang.reshape((1, pos.shape[0]) + (1,) * (x.ndim - 3) + (half,))
    cos, sin = jnp.cos(ang), jnp.sin(ang)
    xf = x.astype(jnp.float32)
    x1, x2, rest = xf[..., :half], xf[..., half:n_rot], xf[..., n_rot:]
    out = jnp.concatenate([x1 * cos - x2 * sin, x2 * cos + x1 * sin, rest], axis=-1)
    return out.astype(x.dtype)


def project(h, w_in, pos, g_q, g_k):
    B, L, _ = h.shape
    z = jnp.einsum('bld,de->ble', h, w_in)
    sizes = (D_RET,) * 4 + (D_DIFF,) * 4
    offs = np.cumsum(sizes)[:-1].tolist()
    rq, rk, rv, rg, dq, dk, dv, dg = jnp.split(z, offs, axis=-1)
    rq = rope(rq.reshape(B, L, H_RET, DK_RET), pos, RET_ROPE_THETA, DK_RET)
    rk = rope(rk.reshape(B, L, H_RET, DK_RET), pos, RET_ROPE_THETA, DK_RET) * (DK_RET ** -0.5)
    rv = rv.reshape(B, L, H_RET, DV_RET)
    dq = rope(rms_norm(dq.reshape(B, L, H_DIFF, 2, DH_DIFF), g_q), pos, ROPE_THETA, ROT_DIFF)
    dk = rope(rms_norm(dk.reshape(B, L, H_DIFF, 2, DH_DIFF), g_k), pos, ROPE_THETA, ROT_DIFF)
    dv = dv.reshape(B, L, H_DIFF, DV_DIFF)
    return rq, rk, rv, rg, dq, dk, dv, dg


def retention_chunk(S, q, k, v):
    L = q.shape[1]
    log_g = jnp.log(1.0 - 2.0 ** (-5.0 - jnp.arange(H_RET, dtype=jnp.float32)))
    idx = jnp.arange(L, dtype=jnp.float32)
    rel = idx[:, None] - idx[None, :]
    dmask = jnp.where(rel[None] >= 0, jnp.exp(log_g[:, None, None] * jnp.maximum(rel, 0.0)[None]), 0.0)
    inner = jnp.einsum('bihd,bjhd->bhij', q, k) * dmask[None]
    o_intra = jnp.einsum('bhij,bjhe->bihe', inner, v)
    q_dec = jnp.exp(log_g[None, :] * (idx[:, None] + 1.0))
    o_cross = jnp.einsum('bihd,bhde->bihe', q * q_dec[None, :, :, None], S)
    k_dec = jnp.exp(log_g[None, :] * (L - 1.0 - idx)[:, None])
    S_new = jnp.exp(log_g * L)[None, :, None, None] * S + jnp.einsum(
        'bjhd,bjhe->bhde', k * k_dec[None, :, :, None], v)
    return o_intra + o_cross, S_new


def retention_prompt(q, k, v):
    B, L = q.shape[0], q.shape[1]
    nc = L // RET_CHUNK
    def to_chunks(t):
        return jnp.moveaxis(t.astype(jnp.float32).reshape((B, nc, RET_CHUNK) + t.shape[2:]), 1, 0)
    S0 = jnp.zeros((B, H_RET, DK_RET, DV_RET), jnp.float32)
    def step(S, qkv):
        o, S = retention_chunk(S, *qkv)
        return S, o
    S_fin, o = lax.scan(step, S0, (to_chunks(q), to_chunks(k), to_chunks(v)))
    o = jnp.moveaxis(o, 0, 1).reshape(B, L, H_RET, DV_RET)
    return o, S_fin


def diff_attend(q, q_pos, k, k_pos, v, lam):
    s = jnp.einsum('bqhcd,bkhcd->bhcqk', q, k).astype(jnp.float32) * (DH_DIFF ** -0.5)
    mask = k_pos[None, :] <= q_pos[:, None]
    s = jnp.where(mask[None, None, None], s, NEG_INF)
    p = jax.nn.softmax(s, axis=-1)
    a = p[:, :, 0] - lam * p[:, :, 1]
    return jnp.einsum('bhqk,bkhe->bqhe', a, v.astype(jnp.float32))


def diff_prompt(q, k, v, lam):
    B, L = q.shape[0], q.shape[1]
    nb = L // Q_BLOCK
    pos = jnp.arange(L)
    qb = jnp.moveaxis(q.reshape((B, nb, Q_BLOCK) + q.shape[2:]), 1, 0)
    pb = pos.reshape(nb, Q_BLOCK)
    o = lax.map(lambda a: diff_attend(a[0], a[1], k, pos, v, lam), (qb, pb))
    return jnp.moveaxis(o, 0, 1).reshape(B, L, H_DIFF, DV_DIFF)


def merge(x, o_ret, o_diff, rg, dg, g_ret_out, g_diff_out, lam_init, w_out):
    B, L, _ = x.shape
    o_ret = rms_norm(o_ret, g_ret_out).astype(x.dtype).reshape(B, L, D_RET)
    o_diff = (rms_norm(o_diff, g_diff_out) * (1.0 - lam_init)).astype(x.dtype).reshape(B, L, D_DIFF)
    u = jnp.concatenate([o_ret * jax.nn.silu(rg), o_diff * jax.nn.silu(dg)], axis=-1)
    return x + jnp.einsum('bld,de->ble', u, w_out)


def setup_inputs(seed: int = 0) -> dict:
    key = jax.random.key(seed)
    ks = jax.random.split(key, 20)
    n_pages = PAST_LEN // PAGE_SIZE
    n_used = DEC_BATCH * n_pages
    n_phys = n_used + n_used // 4
    page_table = jax.random.permutation(ks[0], n_phys)[:n_used].reshape(DEC_BATCH, n_pages).astype(jnp.int32)
    f32 = jnp.float32
    return {
        "x_prompt": jax.random.normal(ks[1], (BATCH, SEQ, D_MODEL), f32),
        "x_sample": jax.random.normal(ks[2], (DEC_BATCH, DEC_SEQ, D_MODEL), f32),
        "cache_k": jax.random.normal(ks[3], (DEPTH, n_phys, PAGE_SIZE, H_DIFF, 2, DH_DIFF), f32),
        "cache_v": jax.random.normal(ks[4], (DEPTH, n_phys, PAGE_SIZE, H_DIFF, DV_DIFF), f32),
        "state_ret": 0.5 * jax.random.normal(ks[5], (DEPTH, DEC_BATCH, H_RET, DK_RET, DV_RET), f32),
        "page_table": page_table,
        "norm_g": 1.0 + 0.01 * jax.random.normal(ks[6], (DEPTH, D_MODEL), f32),
        "w_in": jax.random.normal(ks[7], (DEPTH, D_MODEL, 4 * D_MIX), f32) * D_MODEL ** -0.5,
        "w_out": jax.random.normal(ks[8], (DEPTH, D_MIX, D_MODEL), f32) * D_MIX ** -0.5,
        "q_norm_g": 1.0 + 0.01 * jax.random.normal(ks[9], (DEPTH, DH_DIFF), f32),
        "k_norm_g": 1.0 + 0.01 * jax.random.normal(ks[10], (DEPTH, DH_DIFF), f32),
        "ret_out_g": 1.0 + 0.01 * jax.random.normal(ks[11], (DEPTH, DV_RET), f32),
        "diff_out_g": 1.0 + 0.01 * jax.random.normal(ks[12], (DEPTH, DV_DIFF), f32),
        "lambda_q1": 0.1 * jax.random.normal(ks[13], (DEPTH, DH_DIFF), f32),
        "lambda_k1": 0.1 * jax.random.normal(ks[14], (DEPTH, DH_DIFF), f32),
        "lambda_q2": 0.1 * jax.random.normal(ks[15], (DEPTH, DH_DIFF), f32),
        "lambda_k2": 0.1 * jax.random.normal(ks[16], (DEPTH, DH_DIFF), f32),
    }


def reference(x_prompt, x_sample, cache_k, cache_v, state_ret, page_table, norm_g, w_in, w_out,
              q_norm_g, k_norm_g, ret_out_g, diff_out_g, lambda_q1, lambda_k1, lambda_q2, lambda_k2):
    n_pages = PAST_LEN // PAGE_SIZE
    past = n_pages * PAGE_SIZE
    pos_p = jnp.arange(SEQ)
    pos_s = past + jnp.arange(DEC_SEQ)
    k_pos_s = jnp.arange(past + DEC_SEQ)
    yp, ys = x_prompt, x_sample
    nkp, nvp, nsp, nks, nvs, nss = [], [], [], [], [], []
    for l in range(DEPTH):
        lam_init = 0.8 - 0.6 * math.exp(-0.3 * l)
        lam = (jnp.exp(jnp.sum(lambda_q1[l].astype(jnp.float32) * lambda_k1[l].astype(jnp.float32)))
               - jnp.exp(jnp.sum(lambda_q2[l].astype(jnp.float32) * lambda_k2[l].astype(jnp.float32)))
               + lam_init)
        hp = rms_norm(yp, norm_g[l])
        rq, rk, rv, rg, dq, dk, dv, dg = project(hp, w_in[l], pos_p, q_norm_g[l], k_norm_g[l])
        o_ret, S_p = retention_prompt(rq, rk, rv)
        o_diff = diff_prompt(dq, dk, dv, lam)
        yp = merge(yp, o_ret, o_diff, rg, dg, ret_out_g[l], diff_out_g[l], lam_init, w_out[l])
        nkp.append(dk.astype(cache_k.dtype))
        nvp.append(dv.astype(cache_v.dtype))
        nsp.append(S_p.astype(state_ret.dtype))
        hs = rms_norm(ys, norm_g[l])
        rq, rk, rv, rg, dq, dk, dv, dg = project(hs, w_in[l], pos_s, q_norm_g[l], k_norm_g[l])
        o_ret, S_s = retention_chunk(state_ret[l].astype(jnp.float32), rq.astype(jnp.float32),
                                     rk.astype(jnp.float32), rv.astype(jnp.float32))
        past_k = cache_k[l][page_table].reshape(DEC_BATCH, past, H_DIFF, 2, DH_DIFF)
        past_v = cache_v[l][page_table].reshape(DEC_BATCH, past, H_DIFF, DV_DIFF)
        k_all = jnp.concatenate([past_k.astype(dk.dtype), dk], axis=1)
        v_all = jnp.concatenate([past_v.astype(dv.dtype), dv], axis=1)
        o_diff = diff_attend(dq, pos_s, k_all, k_pos_s, v_all, lam)
        ys = merge(ys, o_ret, o_diff, rg, dg, ret_out_g[l], diff_out_g[l], lam_init, w_out[l])
        nks.append(dk.astype(cache_k.dtype))
        nvs.append(dv.astype(cache_v.dtype))
        nss.append(S_s.astype(state_ret.dtype))
    return (yp, ys, jnp.stack(nkp), jnp.stack(nvp), jnp.stack(nsp),
            jnp.stack(nks), jnp.stack(nvs), jnp.stack(nss))
```

```python
import functools
import math

import jax
import jax.numpy as jnp
from jax import lax
from jax.experimental import pallas as pl
from jax.experimental.pallas import tpu as pltpu

F32 = jnp.float32
BF16 = jnp.bfloat16

D_MODEL = 1024
N_HEADS = 4
D_HEAD = 128
D_GROUP = N_HEADS * D_HEAD
D_MAP = 64
N_ROT_DIFF = 16
ROPE_THETA = 500000.0
RET_ROPE_THETA = 10000.0
PAGE_SIZE = 128
EPS = 1e-6
NEG_INF = -1e30
LAM_INIT = 0.8 - 0.6 * math.exp(-0.3 * 0)
RET_K_SCALE = D_HEAD ** -0.5
DIFF_Q_SCALE = D_MAP ** -0.5
LOG_GAMMA = tuple(math.log(1.0 - 2.0 ** (-5.0 - h)) for h in range(N_HEADS))

LANES = 128
VMEM_LIMIT_BYTES = 56 * 1024 * 1024

TOKEN_TILE = 512
RET_CHUNK = 256
PAGES_PER_STEP = 8

_NT = (((1,), (1,)), ((), ()))
_TN = (((0,), (0,)), ((), ()))


def _head(h):
    return slice(h * D_HEAD, (h + 1) * D_HEAD)


def _silu(g):
    return g * (1.0 / (1.0 + jnp.exp(-g)))


def _rms_rows(x, gain):
    return x * lax.rsqrt(jnp.mean(x * x, axis=-1, keepdims=True) + EPS) * gain


def _rope_full(x, cos_t, sin_t):
    return x * cos_t + pltpu.roll(x, D_HEAD // 2, 1) * sin_t


def _rope_partial(x, c_t, a_t, b_t):
    half = N_ROT_DIFF // 2
    return x * c_t + pltpu.roll(x, LANES - half, 1) * a_t + pltpu.roll(x, half, 1) * b_t


def _map_rms(x, seg, gain):
    ms = jnp.dot((x * x).astype(BF16), seg, preferred_element_type=F32)
    return x * lax.rsqrt(ms + EPS) * gain


def _lambda(lam_ref):
    l = lam_ref[...]
    s1 = jnp.sum(l[0:1] * l[1:2], axis=-1, keepdims=True)
    s2 = jnp.sum(l[2:3] * l[3:4], axis=-1, keepdims=True)
    return jnp.exp(s1) - jnp.exp(s2) + LAM_INIT


def _normed_input(x_ref, ng_ref):
    x = x_ref[...]
    return _rms_rows(x, ng_ref[...]).astype(BF16)


def _z_group(hb, w_ref, g):
    return jnp.dot(hb, w_ref[:, g * D_GROUP:(g + 1) * D_GROUP], preferred_element_type=F32)


def _proj_prompt_kernel(x_ref, ng_ref, w_ref, rc_ref, rs_ref, dc_ref, da_ref, db_ref, gq_ref, gk_ref, seg_ref,
                        rq_ref, rk_ref, rv_ref, rg_ref, dq_ref, dk_ref, dkb_ref, dv_ref, dvt_ref, dg_ref):
    hb = _normed_input(x_ref, ng_ref)
    rc, rs = rc_ref[...], rs_ref[...]
    dc, da, db = dc_ref[...], da_ref[...], db_ref[...]
    seg = seg_ref[...]

    z = _z_group(hb, w_ref, 0)
    for h in range(N_HEADS):
        rq_ref[:, _head(h)] = _rope_full(z[:, _head(h)], rc, rs).astype(BF16)
    z = _z_group(hb, w_ref, 1)
    rck, rsk = rc * RET_K_SCALE, rs * RET_K_SCALE
    for h in range(N_HEADS):
        rk_ref[:, _head(h)] = _rope_full(z[:, _head(h)], rck, rsk).astype(BF16)
    rv_ref[...] = _z_group(hb, w_ref, 2).astype(BF16)
    rg_ref[...] = _z_group(hb, w_ref, 3).astype(BF16)

    z = _z_group(hb, w_ref, 4)
    dcq, daq, dbq = dc * DIFF_Q_SCALE, da * DIFF_Q_SCALE, db * DIFF_Q_SCALE
    for h in range(N_HEADS):
        qn = _map_rms(z[:, _head(h)], seg, gq_ref[...])
        dq_ref[:, _head(h)] = _rope_partial(qn, dcq, daq, dbq).astype(BF16)
    z = _z_group(hb, w_ref, 5)
    for h in range(N_HEADS):
        kn = _map_rms(z[:, _head(h)], seg, gk_ref[...])
        kr = _rope_partial(kn, dc, da, db)
        dk_ref[:, _head(h)] = kr
        dkb_ref[:, _head(h)] = kr.astype(BF16)
    z = _z_group(hb, w_ref, 6)
    dv_ref[...] = z
    dvt_ref[...] = z.T.astype(BF16)
    dg_ref[...] = _z_group(hb, w_ref, 7).astype(BF16)


def _proj_prompt(x, ng, w_bf, tabs, gq, gk, seg):
    B, L, _ = x.shape
    tm = TOKEN_TILE
    nt = L // tm
    tok = lambda width: pl.BlockSpec((None, tm, width), lambda b, i: (b, i, 0))
    tab = pl.BlockSpec((tm, LANES), lambda b, i: (i, 0))
    const = lambda shape: pl.BlockSpec(shape, lambda b, i: (0,) * len(shape))
    bf_out = jax.ShapeDtypeStruct((B, L, D_GROUP), BF16)
    f32_out = jax.ShapeDtypeStruct((B, L, D_GROUP), F32)
    return pl.pallas_call(
        _proj_prompt_kernel,
        grid=(B, nt),
        in_specs=[tok(D_MODEL), const((1, D_MODEL)), const((D_MODEL, 8 * D_GROUP)),
                  tab, tab, tab, tab, tab,
                  const((1, LANES)), const((1, LANES)), const((LANES, LANES))],
        out_specs=[tok(D_GROUP), tok(D_GROUP), tok(D_GROUP), tok(D_GROUP),
                   tok(D_GROUP), tok(D_GROUP), tok(D_GROUP), tok(D_GROUP),
                   pl.BlockSpec((None, None, D_GROUP, tm), lambda b, i: (b, i, 0, 0)),
                   tok(D_GROUP)],
        out_shape=[bf_out, bf_out, bf_out, bf_out, bf_out, f32_out, bf_out, f32_out,
                   jax.ShapeDtypeStruct((B, nt, D_GROUP, tm), BF16), bf_out],
        compiler_params=pltpu.CompilerParams(
            dimension_semantics=("arbitrary", "arbitrary"), vmem_limit_bytes=VMEM_LIMIT_BYTES),
        name="proj_prompt",
    )(x, ng, w_bf, *tabs, gq, gk, seg)


def _ret_prompt_kernel(q_ref, k_ref, v_ref, g_ref, go_ref, u_ref, sfin_ref, s_scr, dm_scr):
    C = q_ref.shape[0]
    b, c = pl.program_id(0), pl.program_id(1)

    @pl.when((b == 0) & (c == 0))
    def _():
        rel = (lax.broadcasted_iota(jnp.int32, (C, C), 0) - lax.broadcasted_iota(jnp.int32, (C, C), 1)).astype(F32)
        for h in range(N_HEADS):
            dm_scr[h] = jnp.where(rel >= 0, jnp.exp(LOG_GAMMA[h] * jnp.maximum(rel, 0.0)), 0.0)

    @pl.when(c == 0)
    def _():
        s_scr[...] = jnp.zeros_like(s_scr)

    idx = lax.broadcasted_iota(jnp.int32, (C, D_HEAD), 0).astype(F32)
    for h in range(N_HEADS):
        lg = LOG_GAMMA[h]
        q, k, v = q_ref[:, _head(h)], k_ref[:, _head(h)], v_ref[:, _head(h)]
        s_old = s_scr[h]
        inner = lax.dot_general(q, k, _NT, preferred_element_type=F32) * dm_scr[h]
        q_dec = (q.astype(F32) * jnp.exp(lg * (idx + 1.0))).astype(BF16)
        o = (jnp.dot(inner.astype(BF16), v, preferred_element_type=F32)
             + jnp.dot(q_dec, s_old.astype(BF16), preferred_element_type=F32))
        k_dec = (k.astype(F32) * jnp.exp(lg * (C - 1.0 - idx))).astype(BF16)
        s_scr[h] = math.exp(lg * C) * s_old + lax.dot_general(k_dec, v, _TN, preferred_element_type=F32)
        u_ref[:, _head(h)] = (_rms_rows(o, go_ref[...]) * _silu(g_ref[:, _head(h)].astype(F32))).astype(BF16)

    @pl.when(c == pl.num_programs(1) - 1)
    def _():
        sfin_ref[...] = s_scr[...]


def _ret_prompt(rq, rk, rv, rg, go):
    B, L, _ = rq.shape
    C = RET_CHUNK
    tok = pl.BlockSpec((None, C, D_GROUP), lambda b, c: (b, c, 0))
    return pl.pallas_call(
        _ret_prompt_kernel,
        grid=(B, L // C),
        in_specs=[tok, tok, tok, tok, pl.BlockSpec((1, D_HEAD), lambda b, c: (0, 0))],
        out_specs=[tok, pl.BlockSpec((None, N_HEADS, D_HEAD, D_HEAD), lambda b, c: (b, 0, 0, 0))],
        out_shape=[jax.ShapeDtypeStruct((B, L, D_GROUP), BF16),
                   jax.ShapeDtypeStruct((B, N_HEADS, D_HEAD, D_HEAD), F32)],
        scratch_shapes=[pltpu.VMEM((N_HEADS, D_HEAD, D_HEAD), F32), pltpu.VMEM((N_HEADS, C, C), F32)],
        compiler_params=pltpu.CompilerParams(
            dimension_semantics=("arbitrary", "arbitrary"), vmem_limit_bytes=VMEM_LIMIT_BYTES),
        name="ret_prompt",
    )(rq, rk, rv, rg, go)


def _diff_prompt_kernel(q_ref, k_ref, vt_ref, g_ref, go_ref, lam_ref, u_ref, acc_scr, m_scr, l_scr):
    tq = q_ref.shape[0]
    tk = vt_ref.shape[-1]
    qi = pl.program_id(2)
    q = q_ref[...]
    lane = lax.broadcasted_iota(jnp.int32, q.shape, 1)
    zero = jnp.zeros_like(q)
    q_both = jnp.concatenate([jnp.where(lane < D_MAP, q, zero), jnp.where(lane >= D_MAP, q, zero)], axis=0)

    m_scr[...] = jnp.full(m_scr.shape, NEG_INF, F32)
    l_scr[...] = jnp.zeros_like(l_scr)
    acc_scr[...] = jnp.zeros_like(acc_scr)

    def step(j, masked):
        k = k_ref[pl.ds(pl.multiple_of(j * tk, tk), tk), :]
        s = lax.dot_general(k, q_both, _NT, preferred_element_type=F32)
        if masked:
            k_pos = j * tk + lax.broadcasted_iota(jnp.int32, s.shape, 0)
            col = lax.broadcasted_iota(jnp.int32, s.shape, 1)
            q_pos = qi * tq + jnp.where(col >= tq, col - tq, col)
            s = jnp.where(k_pos <= q_pos, s, NEG_INF)
        m_old = m_scr[...]
        m_new = jnp.maximum(m_old, jnp.max(s, axis=0, keepdims=True))
        alpha = jnp.exp(m_old - m_new)
        p = jnp.exp(s - m_new)
        l_scr[...] = alpha * l_scr[...] + jnp.sum(p, axis=0, keepdims=True)
        acc_scr[...] = alpha * acc_scr[...] + jnp.dot(vt_ref[j], p.astype(BF16), preferred_element_type=F32)
        m_scr[...] = m_new

    def body(j, carry):
        step(j, masked=False)
        return carry

    lax.fori_loop(0, qi, body, 0)
    step(qi, masked=True)

    acc = acc_scr[...] * (1.0 / l_scr[...])
    o_t = acc[:, :tq] - _lambda(lam_ref) * acc[:, tq:]
    on = _rms_rows(o_t.T, go_ref[...]) * (1.0 - LAM_INIT)
    u_ref[...] = (on * _silu(g_ref[...].astype(F32))).astype(BF16)


def _diff_prompt(dq, dkb, dvt, dg, go, lam_rows):
    B, L, _ = dq.shape
    tq = TOKEN_TILE
    nk, tk = dvt.shape[1], dvt.shape[3]
    assert tq == tk
    tok = pl.BlockSpec((None, tq, D_HEAD), lambda b, h, i: (b, i, h))
    return pl.pallas_call(
        _diff_prompt_kernel,
        grid=(B, N_HEADS, L // tq),
        in_specs=[tok,
                  pl.BlockSpec((None, L, D_HEAD), lambda b, h, i: (b, 0, h)),
                  pl.BlockSpec((None, nk, D_HEAD, tk), lambda b, h, i: (b, 0, h, 0)),
                  tok,
                  pl.BlockSpec((1, D_HEAD), lambda b, h, i: (0, 0)),
                  pl.BlockSpec((4, D_MAP), lambda b, h, i: (0, 0))],
        out_specs=tok,
        out_shape=jax.ShapeDtypeStruct((B, L, D_GROUP), BF16),
        scratch_shapes=[pltpu.VMEM((D_HEAD, 2 * tq), F32), pltpu.VMEM((1, 2 * tq), F32),
                        pltpu.VMEM((1, 2 * tq), F32)],
        compiler_params=pltpu.CompilerParams(
            dimension_semantics=("arbitrary", "arbitrary", "arbitrary"), vmem_limit_bytes=VMEM_LIMIT_BYTES),
        name="diff_prompt",
    )(dq, dkb, dvt, dg, go, lam_rows)


def _out_proj_kernel(x_ref, ur_ref, ud_ref, w_ref, y_ref):
    y_ref[...] = (x_ref[...]
                  + jnp.dot(ur_ref[...].astype(BF16), w_ref[:D_GROUP, :], preferred_element_type=F32)
                  + jnp.dot(ud_ref[...].astype(BF16), w_ref[D_GROUP:, :], preferred_element_type=F32))


def _out_proj(x2d, ur, ud, w_bf, tm):
    T = x2d.shape[0]
    tok = lambda width: pl.BlockSpec((tm, width), lambda i: (i, 0))
    return pl.pallas_call(
        _out_proj_kernel,
        grid=(T // tm,),
        in_specs=[tok(D_MODEL), tok(D_GROUP), tok(D_GROUP),
                  pl.BlockSpec((2 * D_GROUP, D_MODEL), lambda i: (0, 0))],
        out_specs=tok(D_MODEL),
        out_shape=jax.ShapeDtypeStruct((T, D_MODEL), F32),
        compiler_params=pltpu.CompilerParams(
            dimension_semantics=("arbitrary",), vmem_limit_bytes=VMEM_LIMIT_BYTES),
        name="out_proj",
    )(x2d, ur, ud, w_bf)


def _proj_sample_kernel(x_ref, ng_ref, w_ref, rc_ref, rs_ref, dc_ref, da_ref, db_ref, gq_ref, gk_ref, seg_ref,
                        rqt_ref, rkt_ref, rv_ref, rg_ref, dq_ref, dk_ref, dv_ref, dg_ref):
    hb = _normed_input(x_ref, ng_ref)
    n = hb.shape[0]
    rc, rs = rc_ref[...], rs_ref[...]
    dc, da, db = dc_ref[...], da_ref[...], db_ref[...]
    seg = seg_ref[...]
    pad = jnp.zeros((LANES - n, D_HEAD), F32)

    def transposed(xh):
        return jnp.concatenate([xh, pad], axis=0).T

    z = _z_group(hb, w_ref, 0)
    for h in range(N_HEADS):
        rqt_ref[_head(h), :] = transposed(_rope_full(z[:, _head(h)], rc, rs))
    z = _z_group(hb, w_ref, 1)
    rck, rsk = rc * RET_K_SCALE, rs * RET_K_SCALE
    for h in range(N_HEADS):
        rkt_ref[_head(h), :] = transposed(_rope_full(z[:, _head(h)], rck, rsk))
    rv_ref[...] = _z_group(hb, w_ref, 2)
    rg_ref[...] = _z_group(hb, w_ref, 3)
    z = _z_group(hb, w_ref, 4)
    dcq, daq, dbq = dc * DIFF_Q_SCALE, da * DIFF_Q_SCALE, db * DIFF_Q_SCALE
    for h in range(N_HEADS):
        dq_ref[:, _head(h)] = _rope_partial(_map_rms(z[:, _head(h)], seg, gq_ref[...]), dcq, daq, dbq)
    z = _z_group(hb, w_ref, 5)
    for h in range(N_HEADS):
        dk_ref[:, _head(h)] = _rope_partial(_map_rms(z[:, _head(h)], seg, gk_ref[...]), dc, da, db)
    dv_ref[...] = _z_group(hb, w_ref, 6)
    dg_ref[...] = _z_group(hb, w_ref, 7)


def _proj_sample(x, ng, w_bf, tabs, gq, gk, seg):
    n = x.shape[0]
    full = lambda shape: pl.BlockSpec(shape, lambda i: (0,) * len(shape))
    row_out = jax.ShapeDtypeStruct((n, D_GROUP), F32)
    col_out = jax.ShapeDtypeStruct((D_GROUP, LANES), F32)
    return pl.pallas_call(
        _proj_sample_kernel,
        grid=(1,),
        in_specs=[full((n, D_MODEL)), full((1, D_MODEL)), full((D_MODEL, 8 * D_GROUP))]
                 + [full((n, LANES))] * 5 + [full((1, LANES)), full((1, LANES)), full((LANES, LANES))],
        out_specs=[full((D_GROUP, LANES)), full((D_GROUP, LANES))] + [full((n, D_GROUP))] * 6,
        out_shape=[col_out, col_out] + [row_out] * 6,
        compiler_params=pltpu.CompilerParams(
            dimension_semantics=("arbitrary",), vmem_limit_bytes=VMEM_LIMIT_BYTES),
        name="proj_sample",
    )(x, ng, w_bf, *tabs, gq, gk, seg)


def _ret_sample_kernel(qt_ref, kt_ref, v_ref, g_ref, s_ref, go_ref, u_ref, snew_ref):
    b = pl.program_id(0)
    lane = lax.broadcasted_iota(jnp.int32, (D_HEAD, LANES), 1)

    def column(ref, h):
        return jnp.sum(jnp.where(lane == b, ref[_head(h), :], 0.0), axis=1, keepdims=True)

    for h in range(N_HEADS):
        gamma = math.exp(LOG_GAMMA[h])
        q_col, k_col = column(qt_ref, h), column(kt_ref, h)
        v_row = v_ref[:, _head(h)]
        s_old = s_ref[h]
        inner = jnp.sum(q_col * k_col, axis=0, keepdims=True)
        o = inner * v_row + jnp.sum((q_col * gamma) * s_old, axis=0, keepdims=True)
        snew_ref[h] = gamma * s_old + k_col * v_row
        u_ref[:, _head(h)] = _rms_rows(o, go_ref[...]) * _silu(g_ref[:, _head(h)])


def _ret_sample(rqt, rkt, rv3, rg3, state, go):
    n = state.shape[0]
    col = pl.BlockSpec((D_GROUP, LANES), lambda b: (0, 0))
    row = pl.BlockSpec((None, 1, D_GROUP), lambda b: (b, 0, 0))
    st = pl.BlockSpec((None, N_HEADS, D_HEAD, D_HEAD), lambda b: (b, 0, 0, 0))
    return pl.pallas_call(
        _ret_sample_kernel,
        grid=(n,),
        in_specs=[col, col, row, row, st, pl.BlockSpec((1, D_HEAD), lambda b: (0, 0))],
        out_specs=[row, st],
        out_shape=[jax.ShapeDtypeStruct((n, 1, D_GROUP), F32),
                   jax.ShapeDtypeStruct((n, N_HEADS, D_HEAD, D_HEAD), F32)],
        compiler_params=pltpu.CompilerParams(dimension_semantics=("arbitrary",)),
        name="ret_sample",
    )(rqt, rkt, rv3, rg3, state, go)


def _diff_sample_kernel(pt_ref, q_ref, kn_ref, vn_ref, g_ref, go_ref, lam_ref, *rest):
    del pt_ref
    pps = PAGES_PER_STEP
    k_refs, v_refs = rest[:pps], rest[pps:2 * pps]
    u_ref, acc_scr, m_scr, l_scr = rest[2 * pps:]
    s_idx = pl.program_id(1)
    n_maps = 2 * N_HEADS

    q_row = q_ref[...]
    r_idx = lax.broadcasted_iota(jnp.int32, (n_maps, D_GROUP), 0)
    c_idx = lax.broadcasted_iota(jnp.int32, (n_maps, D_GROUP), 1)
    in_map = (c_idx >= r_idx * D_MAP) & (c_idx < (r_idx + 1) * D_MAP)
    q_bd = jnp.where(in_map, jnp.broadcast_to(q_row, (n_maps, D_GROUP)), 0.0)

    @pl.when(s_idx == 0)
    def _():
        m_scr[...] = jnp.full(m_scr.shape, NEG_INF, F32)
        l_scr[...] = jnp.zeros_like(l_scr)
        acc_scr[...] = jnp.zeros_like(acc_scr)

    q_bf = q_bd.astype(BF16)
    s = jnp.concatenate(
        [lax.dot_general(q_bf, k_refs[i][...].astype(BF16), _NT, preferred_element_type=F32) for i in range(pps)],
        axis=1)
    m_old = m_scr[...]
    m_new = jnp.maximum(m_old, jnp.max(s, axis=1, keepdims=True))
    alpha = jnp.exp(m_old - m_new)
    p = jnp.exp(s - m_new)
    l_scr[...] = alpha * l_scr[...] + jnp.sum(p, axis=1, keepdims=True)
    pv = jnp.zeros(acc_scr.shape, F32)
    for i in range(pps):
        pv = pv + jnp.dot(p[:, i * PAGE_SIZE:(i + 1) * PAGE_SIZE].astype(BF16), v_refs[i][...].astype(BF16),
                          preferred_element_type=F32)
    acc_scr[...] = alpha * acc_scr[...] + pv
    m_scr[...] = m_new

    @pl.when(s_idx == pl.num_programs(1) - 1)
    def _():
        s_new = jnp.sum(q_bd * kn_ref[...], axis=1, keepdims=True)
        m_old = m_scr[...]
        m_fin = jnp.maximum(m_old, s_new)
        alpha = jnp.exp(m_old - m_fin)
        p_new = jnp.exp(s_new - m_fin)
        l_fin = alpha * l_scr[...] + p_new
        acc = (alpha * acc_scr[...] + p_new * vn_ref[...]) * (1.0 / l_fin)
        lam = _lambda(lam_ref)
        for h in range(N_HEADS):
            o = acc[2 * h:2 * h + 1, _head(h)] - lam * acc[2 * h + 1:2 * h + 2, _head(h)]
            on = _rms_rows(o, go_ref[...]) * (1.0 - LAM_INIT)
            u_ref[:, _head(h)] = on * _silu(g_ref[:, _head(h)])


def _diff_sample(page_table, dq3, dk3, dv3, dg3, cache_k, cache_v, go, lam_rows):
    n, n_pages = page_table.shape
    pps = PAGES_PER_STEP
    row = pl.BlockSpec((None, 1, D_GROUP), lambda b, s, pt: (b, 0, 0))

    def page(i):
        return pl.BlockSpec((None, PAGE_SIZE, D_GROUP), lambda b, s, pt: (pt[b, s * pps + i], 0, 0))

    n_maps = 2 * N_HEADS
    grid_spec = pltpu.PrefetchScalarGridSpec(
        num_scalar_prefetch=1,
        grid=(n, n_pages // pps),
        in_specs=[row, row, row, row,
                  pl.BlockSpec((1, D_HEAD), lambda b, s, pt: (0, 0)),
                  pl.BlockSpec((4, D_MAP), lambda b, s, pt: (0, 0))]
                 + [page(i) for i in range(pps)] + [page(i) for i in range(pps)],
        out_specs=row,
        scratch_shapes=[pltpu.VMEM((n_maps, D_GROUP), F32), pltpu.VMEM((n_maps, 1), F32),
                        pltpu.VMEM((n_maps, 1), F32)],
    )
    return pl.pallas_call(
        _diff_sample_kernel,
        grid_spec=grid_spec,
        out_shape=jax.ShapeDtypeStruct((n, 1, D_GROUP), F32),
        compiler_params=pltpu.CompilerParams(
            dimension_semantics=("arbitrary", "arbitrary"), vmem_limit_bytes=VMEM_LIMIT_BYTES),
        name="diff_sample",
    )(page_table, dq3, dk3, dv3, dg3, go, lam_rows, *([cache_k] * pps), *([cache_v] * pps))


def _rope_tables(pos):
    posf = pos.astype(F32)[:, None]
    fr = RET_ROPE_THETA ** (-jnp.arange(0, D_HEAD, 2, dtype=F32) / D_HEAD)
    ang = posf * fr[None, :]
    cos, sin = jnp.cos(ang), jnp.sin(ang)
    rc = jnp.concatenate([cos, cos], axis=-1)
    rs = jnp.concatenate([-sin, sin], axis=-1)
    half = N_ROT_DIFF // 2
    fd = ROPE_THETA ** (-jnp.arange(0, N_ROT_DIFF, 2, dtype=F32) / N_ROT_DIFF)
    angd = posf * fd[None, :]
    cd, sd = jnp.cos(angd), jnp.sin(angd)
    n = pos.shape[0]
    ones = jnp.ones((n, D_MAP - N_ROT_DIFF), F32)
    zeros = jnp.zeros((n, D_MAP - N_ROT_DIFF), F32)
    z8 = jnp.zeros((n, half), F32)
    dc = jnp.concatenate([cd, cd, ones], axis=-1)
    da = jnp.concatenate([-sd, z8, zeros], axis=-1)
    db = jnp.concatenate([z8, sd, zeros], axis=-1)
    twice = lambda t: jnp.concatenate([t, t], axis=-1)
    return rc, rs, twice(dc), twice(da), twice(db)


def kernel(x_prompt, x_sample, cache_k, cache_v, state_ret, page_table, norm_g, w_in, w_out,
           q_norm_g, k_norm_g, ret_out_g, diff_out_g, lambda_q1, lambda_k1, lambda_q2, lambda_k2):
    B, L, _ = x_prompt.shape
    n_s = x_sample.shape[0]
    n_pages = page_table.shape[1]
    past = n_pages * PAGE_SIZE
    l = 0

    ng = norm_g[l][None, :]
    w_in_bf = w_in[l].astype(BF16)
    w_out_bf = w_out[l].astype(BF16)
    gq = jnp.tile(q_norm_g[l], 2)[None, :]
    gk = jnp.tile(k_norm_g[l], 2)[None, :]
    go_ret = ret_out_g[l][None, :]
    go_diff = diff_out_g[l][None, :]
    lam_rows = jnp.stack([lambda_q1[l], lambda_k1[l], lambda_q2[l], lambda_k2[l]])
    lane = jnp.arange(LANES)
    seg = jnp.where((lane[:, None] // D_MAP) == (lane[None, :] // D_MAP), 1.0 / D_MAP, 0.0).astype(BF16)

    tabs_p = _rope_tables(jnp.arange(L))
    rq, rk, rv, rg, dq, dk, dkb, dv, dvt, dg = _proj_prompt(x_prompt, ng, w_in_bf, tabs_p, gq, gk, seg)
    u_ret, s_fin = _ret_prompt(rq, rk, rv, rg, go_ret)
    u_diff = _diff_prompt(dq, dkb, dvt, dg, go_diff, lam_rows)
    y_prompt = _out_proj(x_prompt.reshape(B * L, D_MODEL), u_ret.reshape(B * L, D_GROUP),
                         u_diff.reshape(B * L, D_GROUP), w_out_bf, TOKEN_TILE).reshape(B, L, D_MODEL)

    tabs_s = _rope_tables(jnp.full((n_s,), past, jnp.int32))
    rqt, rkt, rv_s, rg_s, dq_s, dk_s, dv_s, dg_s = _proj_sample(
        x_sample.reshape(n_s, D_MODEL), ng, w_in_bf, tabs_s, gq, gk, seg)
    as_rows = lambda t: t.reshape(n_s, 1, D_GROUP)
    u_ret_s, s_new = _ret_sample(rqt, rkt, as_rows(rv_s), as_rows(rg_s), state_ret[l], go_ret)
    n_phys = cache_k.shape[1]
    u_diff_s = _diff_sample(page_table, as_rows(dq_s), as_rows(dk_s), as_rows(dv_s), as_rows(dg_s),
                            cache_k[l].reshape(n_phys, PAGE_SIZE, D_GROUP),
                            cache_v[l].reshape(n_phys, PAGE_SIZE, D_GROUP), go_diff, lam_rows)
    y_sample = _out_proj(x_sample.reshape(n_s, D_MODEL), u_ret_s.reshape(n_s, D_GROUP),
                         u_diff_s.reshape(n_s, D_GROUP), w_out_bf, n_s).reshape(n_s, 1, D_MODEL)

    return (y_prompt, y_sample,
            dk.reshape(1, B, L, N_HEADS, 2, D_MAP), dv.reshape(1, B, L, N_HEADS, D_HEAD), s_fin[None],
            dk_s.reshape(1, n_s, 1, N_HEADS, 2, D_MAP), dv_s.reshape(1, n_s, 1, N_HEADS, D_HEAD), s_new[None])
```

```python
import functools
import math

import jax
import jax.numpy as jnp
from jax import lax
from jax.experimental import pallas as pl
from jax.experimental.pallas import tpu as pltpu

F32 = jnp.float32
BF16 = jnp.bfloat16

D_MODEL = 1024
N_HEADS = 4
D_HEAD = 128
D_GROUP = N_HEADS * D_HEAD
D_MAP = 64
N_ROT_DIFF = 16
ROPE_THETA = 500000.0
RET_ROPE_THETA = 10000.0
PAGE_SIZE = 128
EPS = 1e-6
NEG_INF = -1e30
LAM_INIT = 0.8 - 0.6 * math.exp(-0.3 * 0)
RET_K_SCALE = D_HEAD ** -0.5
DIFF_Q_SCALE = D_MAP ** -0.5 * math.log2(math.e)
LOG_GAMMA = tuple(math.log(1.0 - 2.0 ** (-5.0 - h)) for h in range(N_HEADS))

LANES = 128
VMEM_LIMIT_BYTES = 56 * 1024 * 1024

TOKEN_TILE = 512
Q_CHUNK = 256
ONES_ROWS = 16
VT_ROWS = D_HEAD + ONES_ROWS
RET_CHUNK = 256
PAGES_PER_STEP = 16

_NT = (((1,), (1,)), ((), ()))
_TN = (((0,), (0,)), ((), ()))


def _head(h):
    return slice(h * D_HEAD, (h + 1) * D_HEAD)


def _silu(g):
    return g * (1.0 / (1.0 + jnp.exp(-g)))


def _rms_rows(x, gain):
    return x * lax.rsqrt(jnp.mean(x * x, axis=-1, keepdims=True) + EPS) * gain


def _rope_full(x, cos_t, sin_t):
    return x * cos_t + pltpu.roll(x, D_HEAD // 2, 1) * sin_t


def _rope_partial(x, c_t, a_t, b_t):
    half = N_ROT_DIFF // 2
    return x * c_t + pltpu.roll(x, LANES - half, 1) * a_t + pltpu.roll(x, half, 1) * b_t


def _map_rms(x, seg, gain):
    ms = jnp.dot((x * x).astype(BF16), seg, preferred_element_type=F32)
    return x * lax.rsqrt(ms + EPS) * gain


def _lambda(lam_ref):
    l = lam_ref[...]
    s1 = jnp.sum(l[0:1] * l[1:2], axis=-1, keepdims=True)
    s2 = jnp.sum(l[2:3] * l[3:4], axis=-1, keepdims=True)
    return jnp.exp(s1) - jnp.exp(s2) + LAM_INIT


def _normed_input(x_ref, ng_ref):
    x = x_ref[...]
    return _rms_rows(x, ng_ref[...]).astype(BF16)


def _z_group(hb, w_ref, g):
    return jnp.dot(hb, w_ref[:, g * D_GROUP:(g + 1) * D_GROUP], preferred_element_type=F32)


def _proj_prompt_kernel(x_ref, ng_ref, w_ref, rc_ref, rs_ref, dc_ref, da_ref, db_ref, gq_ref, gk_ref, seg_ref,
                        rq_ref, rk_ref, rv_ref, rg_ref, dq_ref, dk_ref, dkb_ref, dv_ref, dvt_ref, dg_ref):
    hb = _normed_input(x_ref, ng_ref)
    rc, rs = rc_ref[...], rs_ref[...]
    dc, da, db = dc_ref[...], da_ref[...], db_ref[...]
    seg = seg_ref[...]

    z = _z_group(hb, w_ref, 0)
    for h in range(N_HEADS):
        rq_ref[:, _head(h)] = _rope_full(z[:, _head(h)], rc, rs).astype(BF16)
    z = _z_group(hb, w_ref, 1)
    rck, rsk = rc * RET_K_SCALE, rs * RET_K_SCALE
    for h in range(N_HEADS):
        rk_ref[:, _head(h)] = _rope_full(z[:, _head(h)], rck, rsk).astype(BF16)
    rv_ref[...] = _z_group(hb, w_ref, 2).astype(BF16)
    rg_ref[...] = _z_group(hb, w_ref, 3).astype(BF16)

    z = _z_group(hb, w_ref, 4)
    dcq, daq, dbq = dc * DIFF_Q_SCALE, da * DIFF_Q_SCALE, db * DIFF_Q_SCALE
    for h in range(N_HEADS):
        qn = _map_rms(z[:, _head(h)], seg, gq_ref[...])
        dq_ref[:, _head(h)] = _rope_partial(qn, dcq, daq, dbq).astype(BF16)
    z = _z_group(hb, w_ref, 5)
    for h in range(N_HEADS):
        kn = _map_rms(z[:, _head(h)], seg, gk_ref[...])
        kr = _rope_partial(kn, dc, da, db)
        dk_ref[:, _head(h)] = kr
        dkb_ref[:, _head(h)] = kr.astype(BF16)
    z = _z_group(hb, w_ref, 6)
    tm = z.shape[0]
    ones = jnp.ones((ONES_ROWS, tm), BF16)
    for h in range(N_HEADS):
        zh = z[:, _head(h)]
        dv_ref[pl.ds(h, tm, stride=N_HEADS), :] = zh
        dvt_ref[h, :D_HEAD, :] = zh.T.astype(BF16)
        dvt_ref[h, D_HEAD:, :] = ones
    dg_ref[...] = _z_group(hb, w_ref, 7).astype(BF16)


def _proj_prompt(x, ng, w_bf, tabs, gq, gk, seg):
    B, L, _ = x.shape
    tm = TOKEN_TILE
    nt = L // tm
    tok = lambda width: pl.BlockSpec((None, tm, width), lambda b, i: (b, i, 0))
    tab = pl.BlockSpec((tm, LANES), lambda b, i: (i, 0))
    const = lambda shape: pl.BlockSpec(shape, lambda b, i: (0,) * len(shape))
    bf_out = jax.ShapeDtypeStruct((B, L, D_GROUP), BF16)
    f32_out = jax.ShapeDtypeStruct((B, L, D_GROUP), F32)
    return pl.pallas_call(
        _proj_prompt_kernel,
        grid=(B, nt),
        in_specs=[tok(D_MODEL), const((1, D_MODEL)), const((D_MODEL, 8 * D_GROUP)),
                  tab, tab, tab, tab, tab,
                  const((1, LANES)), const((1, LANES)), const((LANES, LANES))],
        out_specs=[tok(D_GROUP), tok(D_GROUP), tok(D_GROUP), tok(D_GROUP),
                   tok(D_GROUP), tok(D_GROUP), tok(D_GROUP),
                   pl.BlockSpec((None, tm * N_HEADS, D_HEAD), lambda b, i: (b, i, 0)),
                   pl.BlockSpec((None, None, N_HEADS, VT_ROWS, tm), lambda b, i: (b, i, 0, 0, 0)),
                   tok(D_GROUP)],
        out_shape=[bf_out, bf_out, bf_out, bf_out, bf_out, f32_out, bf_out,
                   jax.ShapeDtypeStruct((B, L * N_HEADS, D_HEAD), F32),
                   jax.ShapeDtypeStruct((B, nt, N_HEADS, VT_ROWS, tm), BF16), bf_out],
        compiler_params=pltpu.CompilerParams(
            dimension_semantics=("arbitrary", "arbitrary"), vmem_limit_bytes=VMEM_LIMIT_BYTES),
        name="proj_prompt",
    )(x, ng, w_bf, *tabs, gq, gk, seg)


def _ret_prompt_kernel(q_ref, k_ref, v_ref, g_ref, go_ref, u_ref, sfin_ref, s_scr, dm_scr):
    C = q_ref.shape[0]
    b, c = pl.program_id(0), pl.program_id(1)

    @pl.when((b == 0) & (c == 0))
    def _():
        rel = (lax.broadcasted_iota(jnp.int32, (C, C), 0) - lax.broadcasted_iota(jnp.int32, (C, C), 1)).astype(F32)
        for h in range(N_HEADS):
            dm_scr[h] = jnp.where(rel >= 0, jnp.exp(LOG_GAMMA[h] * jnp.maximum(rel, 0.0)), 0.0)

    @pl.when(c == 0)
    def _():
        s_scr[...] = jnp.zeros_like(s_scr)

    idx = lax.broadcasted_iota(jnp.int32, (C, D_HEAD), 0).astype(F32)
    for h in range(N_HEADS):
        lg = LOG_GAMMA[h]
        q, k, v = q_ref[:, _head(h)], k_ref[:, _head(h)], v_ref[:, _head(h)]
        s_old = s_scr[h]
        inner = lax.dot_general(q, k, _NT, preferred_element_type=F32) * dm_scr[h]
        q_dec = (q.astype(F32) * jnp.exp(lg * (idx + 1.0))).astype(BF16)
        o = (jnp.dot(inner.astype(BF16), v, preferred_element_type=F32)
             + jnp.dot(q_dec, s_old.astype(BF16), preferred_element_type=F32))
        k_dec = (k.astype(F32) * jnp.exp(lg * (C - 1.0 - idx))).astype(BF16)
        s_scr[h] = math.exp(lg * C) * s_old + lax.dot_general(k_dec, v, _TN, preferred_element_type=F32)
        u_ref[:, _head(h)] = (_rms_rows(o, go_ref[...]) * _silu(g_ref[:, _head(h)].astype(F32))).astype(BF16)

    @pl.when(c == pl.num_programs(1) - 1)
    def _():
        sfin_ref[...] = s_scr[...]


def _ret_prompt(rq, rk, rv, rg, go):
    B, L, _ = rq.shape
    C = RET_CHUNK
    tok = pl.BlockSpec((None, C, D_GROUP), lambda b, c: (b, c, 0))
    return pl.pallas_call(
        _ret_prompt_kernel,
        grid=(B, L // C),
        in_specs=[tok, tok, tok, tok, pl.BlockSpec((1, D_HEAD), lambda b, c: (0, 0))],
        out_specs=[tok, pl.BlockSpec((None, N_HEADS, D_HEAD, D_HEAD), lambda b, c: (b, 0, 0, 0))],
        out_shape=[jax.ShapeDtypeStruct((B, L, D_GROUP), BF16),
                   jax.ShapeDtypeStruct((B, N_HEADS, D_HEAD, D_HEAD), F32)],
        scratch_shapes=[pltpu.VMEM((N_HEADS, D_HEAD, D_HEAD), F32), pltpu.VMEM((N_HEADS, C, C), F32)],
        compiler_params=pltpu.CompilerParams(
            dimension_semantics=("arbitrary", "arbitrary"), vmem_limit_bytes=VMEM_LIMIT_BYTES),
        name="ret_prompt",
    )(rq, rk, rv, rg, go)


def _diff_prompt_kernel(q_ref, k_ref, vt_ref, g_ref, go_ref, lam_ref, u_ref, acc_scr, m_scr):
    tq = q_ref.shape[0]
    tk = vt_ref.shape[-1]
    n_half = tq // Q_CHUNK
    qi = pl.program_id(2)
    q = q_ref[...]
    lane = lax.broadcasted_iota(jnp.int32, q.shape, 1)
    zero = jnp.zeros_like(q)
    q_maps = (jnp.where(lane < D_MAP, q, zero), jnp.where(lane >= D_MAP, q, zero))
    chunks = [(mp, hf) for mp in range(2) for hf in range(n_half)]

    m_scr[...] = jnp.full(m_scr.shape, NEG_INF, F32)
    acc_scr[...] = jnp.zeros_like(acc_scr)

    def step(j, diagonal):
        base = pl.multiple_of(j * tk, tk)
        n_rows = [(hf + 1) * Q_CHUNK if diagonal else tk for _, hf in chunks]
        m_all, acc_all = m_scr[...], acc_scr[...]
        scores = []
        for (mp, hf), rows in zip(chunks, n_rows):
            qc = q_maps[mp][hf * Q_CHUNK:(hf + 1) * Q_CHUNK]
            s = lax.dot_general(k_ref[pl.ds(base, rows), :], qc, _NT, preferred_element_type=F32)
            if diagonal:
                k_pos = lax.broadcasted_iota(jnp.int32, s.shape, 0)
                q_pos = hf * Q_CHUNK + lax.broadcasted_iota(jnp.int32, s.shape, 1)
                s = jnp.where(k_pos <= q_pos, s, NEG_INF)
            scores.append(s)
        m_out, acc_out = [], []
        for c, (s, rows) in enumerate(zip(scores, n_rows)):
            cols = slice(c * Q_CHUNK, (c + 1) * Q_CHUNK)
            m_old = m_all[:, cols]
            m_new = jnp.maximum(m_old, jnp.max(s, axis=0, keepdims=True))
            alpha = jnp.exp2(m_old - m_new)
            p = jnp.exp2((s - m_new).astype(BF16))
            pv = jnp.dot(vt_ref[j, :, :rows], p, preferred_element_type=F32)
            acc_out.append(alpha * acc_all[:, cols] + pv)
            m_out.append(m_new)
        m_scr[...] = jnp.concatenate(m_out, axis=1)
        acc_scr[...] = jnp.concatenate(acc_out, axis=1)

    def body(j, carry):
        step(j, diagonal=False)
        return carry

    lax.fori_loop(0, qi, body, 0)
    step(qi, diagonal=True)

    acc = acc_scr[...]
    o_all = acc[:D_HEAD] * (1.0 / acc[D_HEAD:D_HEAD + 1])
    o_t = o_all[:, :tq] - _lambda(lam_ref) * o_all[:, tq:]
    on = _rms_rows(o_t.T, go_ref[...]) * (1.0 - LAM_INIT)
    u_ref[...] = (on * _silu(g_ref[...].astype(F32))).astype(BF16)


def _diff_prompt(dq, dkb, dvt, dg, go, lam_rows):
    B, L, _ = dq.shape
    tq = TOKEN_TILE
    nk, tk = dvt.shape[1], dvt.shape[4]
    assert tq == tk and tq % Q_CHUNK == 0
    tok = pl.BlockSpec((None, tq, D_HEAD), lambda b, h, i: (b, i, h))
    return pl.pallas_call(
        _diff_prompt_kernel,
        grid=(B, N_HEADS, L // tq),
        in_specs=[tok,
                  pl.BlockSpec((None, L, D_HEAD), lambda b, h, i: (b, 0, h)),
                  pl.BlockSpec((None, nk, None, VT_ROWS, tk), lambda b, h, i: (b, 0, h, 0, 0)),
                  tok,
                  pl.BlockSpec((1, D_HEAD), lambda b, h, i: (0, 0)),
                  pl.BlockSpec((4, D_MAP), lambda b, h, i: (0, 0))],
        out_specs=tok,
        out_shape=jax.ShapeDtypeStruct((B, L, D_GROUP), BF16),
        scratch_shapes=[pltpu.VMEM((VT_ROWS, 2 * tq), F32), pltpu.VMEM((1, 2 * tq), F32)],
        compiler_params=pltpu.CompilerParams(
            dimension_semantics=("arbitrary", "arbitrary", "arbitrary"), vmem_limit_bytes=VMEM_LIMIT_BYTES),
        name="diff_prompt",
    )(dq, dkb, dvt, dg, go, lam_rows)


def _out_proj_kernel(x_ref, ur_ref, ud_ref, w_ref, y_ref):
    y_ref[...] = (x_ref[...]
                  + jnp.dot(ur_ref[...].astype(BF16), w_ref[:D_GROUP, :], preferred_element_type=F32)
                  + jnp.dot(ud_ref[...].astype(BF16), w_ref[D_GROUP:, :], preferred_element_type=F32))


def _out_proj(x2d, ur, ud, w_bf, tm):
    T = x2d.shape[0]
    tok = lambda width: pl.BlockSpec((tm, width), lambda i: (i, 0))
    return pl.pallas_call(
        _out_proj_kernel,
        grid=(T // tm,),
        in_specs=[tok(D_MODEL), tok(D_GROUP), tok(D_GROUP),
                  pl.BlockSpec((2 * D_GROUP, D_MODEL), lambda i: (0, 0))],
        out_specs=tok(D_MODEL),
        out_shape=jax.ShapeDtypeStruct((T, D_MODEL), F32),
        compiler_params=pltpu.CompilerParams(
            dimension_semantics=("arbitrary",), vmem_limit_bytes=VMEM_LIMIT_BYTES),
        name="out_proj",
    )(x2d, ur, ud, w_bf)


def _proj_sample_kernel(x_ref, ng_ref, w_ref, rc_ref, rs_ref, dc_ref, da_ref, db_ref, gq_ref, gk_ref, seg_ref,
                        rqt_ref, rkt_ref, dqt_ref, rv_ref, rg_ref, dq_ref, dk_ref, dv_ref, dg_ref):
    hb = _normed_input(x_ref, ng_ref)
    n = hb.shape[0]
    rc, rs = rc_ref[...], rs_ref[...]
    dc, da, db = dc_ref[...], da_ref[...], db_ref[...]
    seg = seg_ref[...]
    pad = jnp.zeros((LANES - n, D_HEAD), F32)

    def transposed(xh):
        return jnp.concatenate([xh, pad], axis=0).T

    z = _z_group(hb, w_ref, 0)
    for h in range(N_HEADS):
        rqt_ref[_head(h), :] = transposed(_rope_full(z[:, _head(h)], rc, rs))
    z = _z_group(hb, w_ref, 1)
    rck, rsk = rc * RET_K_SCALE, rs * RET_K_SCALE
    for h in range(N_HEADS):
        rkt_ref[_head(h), :] = transposed(_rope_full(z[:, _head(h)], rck, rsk))
    rv_ref[...] = _z_group(hb, w_ref, 2)
    rg_ref[...] = _z_group(hb, w_ref, 3)
    z = _z_group(hb, w_ref, 4)
    dcq, daq, dbq = dc * DIFF_Q_SCALE, da * DIFF_Q_SCALE, db * DIFF_Q_SCALE
    for h in range(N_HEADS):
        qh = _rope_partial(_map_rms(z[:, _head(h)], seg, gq_ref[...]), dcq, daq, dbq)
        dq_ref[:, _head(h)] = qh
        dqt_ref[_head(h), :] = transposed(qh)
    z = _z_group(hb, w_ref, 5)
    for h in range(N_HEADS):
        dk_ref[:, _head(h)] = _rope_partial(_map_rms(z[:, _head(h)], seg, gk_ref[...]), dc, da, db)
    dv_ref[...] = _z_group(hb, w_ref, 6)
    dg_ref[...] = _z_group(hb, w_ref, 7)


def _proj_sample(x, ng, w_bf, tabs, gq, gk, seg):
    n = x.shape[0]
    full = lambda shape: pl.BlockSpec(shape, lambda i: (0,) * len(shape))
    row_out = jax.ShapeDtypeStruct((n, D_GROUP), F32)
    col_out = jax.ShapeDtypeStruct((D_GROUP, LANES), F32)
    return pl.pallas_call(
        _proj_sample_kernel,
        grid=(1,),
        in_specs=[full((n, D_MODEL)), full((1, D_MODEL)), full((D_MODEL, 8 * D_GROUP))]
                 + [full((n, LANES))] * 5 + [full((1, LANES)), full((1, LANES)), full((LANES, LANES))],
        out_specs=[full((D_GROUP, LANES))] * 3 + [full((n, D_GROUP))] * 6,
        out_shape=[col_out] * 3 + [row_out] * 6,
        compiler_params=pltpu.CompilerParams(
            dimension_semantics=("arbitrary",), vmem_limit_bytes=VMEM_LIMIT_BYTES),
        name="proj_sample",
    )(x, ng, w_bf, *tabs, gq, gk, seg)


def _ret_sample_kernel(qt_ref, kt_ref, v_ref, g_ref, s_ref, go_ref, u_ref, snew_ref):
    b = pl.program_id(0)
    lane = lax.broadcasted_iota(jnp.int32, (D_HEAD, LANES), 1)

    def column(ref, h):
        return jnp.sum(jnp.where(lane == b, ref[_head(h), :], 0.0), axis=1, keepdims=True)

    for h in range(N_HEADS):
        gamma = math.exp(LOG_GAMMA[h])
        q_col, k_col = column(qt_ref, h), column(kt_ref, h)
        v_row = v_ref[:, _head(h)]
        s_old = s_ref[h]
        inner = jnp.sum(q_col * k_col, axis=0, keepdims=True)
        o = inner * v_row + jnp.sum((q_col * gamma) * s_old, axis=0, keepdims=True)
        snew_ref[h] = gamma * s_old + k_col * v_row
        u_ref[:, _head(h)] = _rms_rows(o, go_ref[...]) * _silu(g_ref[:, _head(h)])


def _ret_sample(rqt, rkt, rv3, rg3, state, go):
    n = state.shape[0]
    col = pl.BlockSpec((D_GROUP, LANES), lambda b: (0, 0))
    row = pl.BlockSpec((None, 1, D_GROUP), lambda b: (b, 0, 0))
    st = pl.BlockSpec((None, N_HEADS, D_HEAD, D_HEAD), lambda b: (b, 0, 0, 0))
    return pl.pallas_call(
        _ret_sample_kernel,
        grid=(n,),
        in_specs=[col, col, row, row, st, pl.BlockSpec((1, D_HEAD), lambda b: (0, 0))],
        out_specs=[row, st],
        out_shape=[jax.ShapeDtypeStruct((n, 1, D_GROUP), F32),
                   jax.ShapeDtypeStruct((n, N_HEADS, D_HEAD, D_HEAD), F32)],
        compiler_params=pltpu.CompilerParams(dimension_semantics=("arbitrary",)),
        name="ret_sample",
    )(rqt, rkt, rv3, rg3, state, go)


def _diff_sample_kernel(pt_ref, qt_ref, q_ref, kn_ref, vn_ref, g_ref, go_ref, lam_ref, *rest):
    del pt_ref
    pps = PAGES_PER_STEP
    k_refs, v_refs = rest[:pps], rest[pps:2 * pps]
    u_ref, qb_scr, acc_scr, m_scr, l_scr = rest[2 * pps:]
    b, s_idx = pl.program_id(0), pl.program_id(1)
    n_maps = 2 * N_HEADS

    @pl.when(s_idx == 0)
    def _():
        lane = lax.broadcasted_iota(jnp.int32, (D_GROUP, LANES), 1)
        q_col = jnp.sum(jnp.where(lane == b, qt_ref[...], 0.0), axis=1, keepdims=True)
        qb_scr[...] = jnp.broadcast_to(q_col, (D_GROUP, LANES))
        m_scr[...] = jnp.full(m_scr.shape, NEG_INF, F32)
        l_scr[...] = jnp.zeros_like(l_scr)
        acc_scr[...] = jnp.zeros_like(acc_scr)

    row_id = lax.broadcasted_iota(jnp.int32, (n_maps, PAGE_SIZE), 0)
    s_pages = []
    for i in range(pps):
        s_i = jnp.zeros((n_maps, PAGE_SIZE), F32)
        for r in range(n_maps):
            rows = slice(r * D_MAP, (r + 1) * D_MAP)
            s_r = jnp.sum(k_refs[i][rows, :] * qb_scr[rows, :], axis=0, keepdims=True)
            s_i = jnp.where(row_id == r, s_r, s_i)
        s_pages.append(s_i)
    s = jnp.concatenate(s_pages, axis=1)
    m_old = m_scr[...]
    m_new = jnp.maximum(m_old, jnp.max(s, axis=1, keepdims=True))
    alpha = jnp.exp2(m_old - m_new)
    p = jnp.exp2(s - m_new)
    l_scr[...] = alpha * l_scr[...] + jnp.sum(p, axis=1, keepdims=True)
    p_bf = p.astype(BF16)
    pv = []
    for h in range(N_HEADS):
        v_h = jnp.concatenate([v_refs[i][pl.ds(h, PAGE_SIZE, stride=N_HEADS), :] for i in range(pps)], axis=0)
        pv.append(jnp.dot(p_bf, v_h.astype(BF16), preferred_element_type=F32))
    acc_scr[...] = alpha * acc_scr[...] + jnp.concatenate(pv, axis=1)
    m_scr[...] = m_new

    @pl.when(s_idx == pl.num_programs(1) - 1)
    def _():
        r_idx = lax.broadcasted_iota(jnp.int32, (n_maps, D_GROUP), 0)
        c_idx = lax.broadcasted_iota(jnp.int32, (n_maps, D_GROUP), 1)
        in_map = (c_idx >= r_idx * D_MAP) & (c_idx < (r_idx + 1) * D_MAP)
        q_bd = jnp.where(in_map, jnp.broadcast_to(q_ref[...], (n_maps, D_GROUP)), 0.0)
        s_new = jnp.sum(q_bd * kn_ref[...], axis=1, keepdims=True)
        m_old = m_scr[...]
        m_fin = jnp.maximum(m_old, s_new)
        alpha = jnp.exp2(m_old - m_fin)
        p_new = jnp.exp2(s_new - m_fin)
        l_fin = alpha * l_scr[...] + p_new
        acc = (alpha * acc_scr[...] + p_new * vn_ref[...]) * (1.0 / l_fin)
        lam = _lambda(lam_ref)
        for h in range(N_HEADS):
            o = acc[2 * h:2 * h + 1, _head(h)] - lam * acc[2 * h + 1:2 * h + 2, _head(h)]
            on = _rms_rows(o, go_ref[...]) * (1.0 - LAM_INIT)
            u_ref[:, _head(h)] = on * _silu(g_ref[:, _head(h)])


def _diff_sample(page_table, dqt, dq3, dk3, dv3, dg3, k_pages, v_pages, go, lam_rows):
    n, n_pages = page_table.shape
    pps = PAGES_PER_STEP
    row = pl.BlockSpec((None, 1, D_GROUP), lambda b, s, pt: (b, 0, 0))

    def page(i):
        return pl.BlockSpec((None, D_GROUP, LANES), lambda b, s, pt: (pt[b, s * pps + i], 0, 0))

    n_maps = 2 * N_HEADS
    grid_spec = pltpu.PrefetchScalarGridSpec(
        num_scalar_prefetch=1,
        grid=(n, n_pages // pps),
        in_specs=[pl.BlockSpec((D_GROUP, LANES), lambda b, s, pt: (0, 0)), row, row, row, row,
                  pl.BlockSpec((1, D_HEAD), lambda b, s, pt: (0, 0)),
                  pl.BlockSpec((4, D_MAP), lambda b, s, pt: (0, 0))]
                 + [page(i) for i in range(pps)] + [page(i) for i in range(pps)],
        out_specs=row,
        scratch_shapes=[pltpu.VMEM((D_GROUP, LANES), F32), pltpu.VMEM((n_maps, D_GROUP), F32),
                        pltpu.VMEM((n_maps, 1), F32), pltpu.VMEM((n_maps, 1), F32)],
    )
    return pl.pallas_call(
        _diff_sample_kernel,
        grid_spec=grid_spec,
        out_shape=jax.ShapeDtypeStruct((n, 1, D_GROUP), F32),
        compiler_params=pltpu.CompilerParams(
            dimension_semantics=("arbitrary", "arbitrary"), vmem_limit_bytes=VMEM_LIMIT_BYTES),
        name="diff_sample",
    )(page_table, dqt, dq3, dk3, dv3, dg3, go, lam_rows, *([k_pages] * pps), *([v_pages] * pps))


def _rope_tables(pos):
    posf = pos.astype(F32)[:, None]
    fr = RET_ROPE_THETA ** (-jnp.arange(0, D_HEAD, 2, dtype=F32) / D_HEAD)
    ang = posf * fr[None, :]
    cos, sin = jnp.cos(ang), jnp.sin(ang)
    rc = jnp.concatenate([cos, cos], axis=-1)
    rs = jnp.concatenate([-sin, sin], axis=-1)
    half = N_ROT_DIFF // 2
    fd = ROPE_THETA ** (-jnp.arange(0, N_ROT_DIFF, 2, dtype=F32) / N_ROT_DIFF)
    angd = posf * fd[None, :]
    cd, sd = jnp.cos(angd), jnp.sin(angd)
    n = pos.shape[0]
    ones = jnp.ones((n, D_MAP - N_ROT_DIFF), F32)
    zeros = jnp.zeros((n, D_MAP - N_ROT_DIFF), F32)
    z8 = jnp.zeros((n, half), F32)
    dc = jnp.concatenate([cd, cd, ones], axis=-1)
    da = jnp.concatenate([-sd, z8, zeros], axis=-1)
    db = jnp.concatenate([z8, sd, zeros], axis=-1)
    twice = lambda t: jnp.concatenate([t, t], axis=-1)
    return rc, rs, twice(dc), twice(da), twice(db)


def kernel(x_prompt, x_sample, cache_k, cache_v, state_ret, page_table, norm_g, w_in, w_out,
           q_norm_g, k_norm_g, ret_out_g, diff_out_g, lambda_q1, lambda_k1, lambda_q2, lambda_k2):
    B, L, _ = x_prompt.shape
    n_s = x_sample.shape[0]
    n_pages = page_table.shape[1]
    past = n_pages * PAGE_SIZE
    l = 0

    ng = norm_g[l][None, :]
    w_in_bf = w_in[l].astype(BF16)
    w_out_bf = w_out[l].astype(BF16)
    gq = jnp.tile(q_norm_g[l], 2)[None, :]
    gk = jnp.tile(k_norm_g[l], 2)[None, :]
    go_ret = ret_out_g[l][None, :]
    go_diff = diff_out_g[l][None, :]
    lam_rows = jnp.stack([lambda_q1[l], lambda_k1[l], lambda_q2[l], lambda_k2[l]])
    lane = jnp.arange(LANES)
    seg = jnp.where((lane[:, None] // D_MAP) == (lane[None, :] // D_MAP), 1.0 / D_MAP, 0.0).astype(BF16)

    tabs_p = _rope_tables(jnp.arange(L))
    rq, rk, rv, rg, dq, dk, dkb, dv, dvt, dg = _proj_prompt(x_prompt, ng, w_in_bf, tabs_p, gq, gk, seg)
    u_ret, s_fin = _ret_prompt(rq, rk, rv, rg, go_ret)
    u_diff = _diff_prompt(dq, dkb, dvt, dg, go_diff, lam_rows)
    y_prompt = _out_proj(x_prompt.reshape(B * L, D_MODEL), u_ret.reshape(B * L, D_GROUP),
                         u_diff.reshape(B * L, D_GROUP), w_out_bf, TOKEN_TILE).reshape(B, L, D_MODEL)

    tabs_s = _rope_tables(jnp.full((n_s,), past, jnp.int32))
    rqt, rkt, dqt, rv_s, rg_s, dq_s, dk_s, dv_s, dg_s = _proj_sample(
        x_sample.reshape(n_s, D_MODEL), ng, w_in_bf, tabs_s, gq, gk, seg)
    as_rows = lambda t: t.reshape(n_s, 1, D_GROUP)
    u_ret_s, s_new = _ret_sample(rqt, rkt, as_rows(rv_s), as_rows(rg_s), state_ret[l], go_ret)
    n_phys = cache_k.shape[1]
    k_pages = jnp.transpose(cache_k[l], (0, 2, 3, 4, 1)).reshape(n_phys, D_GROUP, PAGE_SIZE)
    v_pages = cache_v[l].reshape(n_phys, PAGE_SIZE * N_HEADS, D_HEAD)
    u_diff_s = _diff_sample(page_table, dqt, as_rows(dq_s), as_rows(dk_s), as_rows(dv_s), as_rows(dg_s),
                            k_pages, v_pages, go_diff, lam_rows)
    y_sample = _out_proj(x_sample.reshape(n_s, D_MODEL), u_ret_s.reshape(n_s, D_GROUP),
                         u_diff_s.reshape(n_s, D_GROUP), w_out_bf, n_s).reshape(n_s, 1, D_MODEL)

    return (y_prompt, y_sample,
            dk.reshape(1, B, L, N_HEADS, 2, D_MAP), dv.reshape(1, B, L, N_HEADS, D_HEAD), s_fin[None],
            dk_s.reshape(1, n_s, 1, N_HEADS, 2, D_MAP), dv_s.reshape(1, n_s, 1, N_HEADS, D_HEAD), s_new[None])
```

```python
import functools
import math

import jax
import jax.numpy as jnp
from jax import lax
from jax.experimental import pallas as pl
from jax.experimental.pallas import tpu as pltpu

F32 = jnp.float32
BF16 = jnp.bfloat16

D_MODEL = 1024
N_HEADS = 4
D_HEAD = 128
D_GROUP = N_HEADS * D_HEAD
D_MAP = 64
N_ROT_DIFF = 16
ROPE_THETA = 500000.0
RET_ROPE_THETA = 10000.0
PAGE_SIZE = 128
EPS = 1e-6
NEG_INF = -1e30
LAM_INIT = 0.8 - 0.6 * math.exp(-0.3 * 0)
RET_K_SCALE = D_HEAD ** -0.5
DIFF_Q_SCALE = D_MAP ** -0.5 * math.log2(math.e)
LOG_GAMMA = tuple(math.log(1.0 - 2.0 ** (-5.0 - h)) for h in range(N_HEADS))

LANES = 128
VMEM_LIMIT_BYTES = 56 * 1024 * 1024

TOKEN_TILE = 512
Q_CHUNK = 256
ONES_ROWS = 16
VT_ROWS = D_HEAD + ONES_ROWS
RET_CHUNK = 256
PAGES_PER_STEP = 16

_NT = (((1,), (1,)), ((), ()))
_TN = (((0,), (0,)), ((), ()))


def _head(h):
    return slice(h * D_HEAD, (h + 1) * D_HEAD)


def _silu(g):
    return g * (1.0 / (1.0 + jnp.exp(-g)))


def _rms_rows(x, gain):
    return x * lax.rsqrt(jnp.mean(x * x, axis=-1, keepdims=True) + EPS) * gain


def _rope_full(x, cos_t, sin_t):
    return x * cos_t + pltpu.roll(x, D_HEAD // 2, 1) * sin_t


def _rope_partial(x, c_t, a_t, b_t):
    half = N_ROT_DIFF // 2
    return x * c_t + pltpu.roll(x, LANES - half, 1) * a_t + pltpu.roll(x, half, 1) * b_t


def _map_rms(x, seg, gain):
    ms = jnp.dot((x * x).astype(BF16), seg, preferred_element_type=F32)
    return x * lax.rsqrt(ms + EPS) * gain


def _lambda(lam_ref):
    l = lam_ref[...]
    s1 = jnp.sum(l[0:1] * l[1:2], axis=-1, keepdims=True)
    s2 = jnp.sum(l[2:3] * l[3:4], axis=-1, keepdims=True)
    return jnp.exp(s1) - jnp.exp(s2) + LAM_INIT


def _normed_input(x_ref, ng_ref):
    x = x_ref[...]
    return _rms_rows(x, ng_ref[...]).astype(BF16)


def _z_group(hb, w_ref, g):
    return jnp.dot(hb, w_ref[:, g * D_GROUP:(g + 1) * D_GROUP], preferred_element_type=F32)


def _proj_prompt_kernel(x_ref, ng_ref, w_ref, rc_ref, rs_ref, dc_ref, da_ref, db_ref, gq_ref, gk_ref, seg_ref,
                        rq_ref, rk_ref, rv_ref, rg_ref, dq_ref, dk_ref, dkb_ref, dv_ref, dvt_ref, dg_ref):
    hb = _normed_input(x_ref, ng_ref)
    rc, rs = rc_ref[...], rs_ref[...]
    dc, da, db = dc_ref[...], da_ref[...], db_ref[...]
    seg = seg_ref[...]

    z = _z_group(hb, w_ref, 0)
    for h in range(N_HEADS):
        rq_ref[:, _head(h)] = _rope_full(z[:, _head(h)], rc, rs).astype(BF16)
    z = _z_group(hb, w_ref, 1)
    rck, rsk = rc * RET_K_SCALE, rs * RET_K_SCALE
    for h in range(N_HEADS):
        rk_ref[:, _head(h)] = _rope_full(z[:, _head(h)], rck, rsk).astype(BF16)
    rv_ref[...] = _z_group(hb, w_ref, 2).astype(BF16)
    rg_ref[...] = _z_group(hb, w_ref, 3).astype(BF16)

    z = _z_group(hb, w_ref, 4)
    dcq, daq, dbq = dc * DIFF_Q_SCALE, da * DIFF_Q_SCALE, db * DIFF_Q_SCALE
    for h in range(N_HEADS):
        qn = _map_rms(z[:, _head(h)], seg, gq_ref[...])
        dq_ref[:, _head(h)] = _rope_partial(qn, dcq, daq, dbq).astype(BF16)
    z = _z_group(hb, w_ref, 5)
    for h in range(N_HEADS):
        kn = _map_rms(z[:, _head(h)], seg, gk_ref[...])
        kr = _rope_partial(kn, dc, da, db)
        dk_ref[:, _head(h)] = kr
        dkb_ref[:, _head(h)] = kr.astype(BF16)
    z = _z_group(hb, w_ref, 6)
    tm = z.shape[0]
    ones = jnp.ones((ONES_ROWS, tm), BF16)
    for h in range(N_HEADS):
        zh = z[:, _head(h)]
        dv_ref[pl.ds(h, tm, stride=N_HEADS), :] = zh
        dvt_ref[h, :D_HEAD, :] = zh.T.astype(BF16)
        dvt_ref[h, D_HEAD:, :] = ones
    dg_ref[...] = _z_group(hb, w_ref, 7).astype(BF16)


def _proj_prompt(x, ng, w_bf, tabs, gq, gk, seg):
    B, L, _ = x.shape
    tm = TOKEN_TILE
    nt = L // tm
    tok = lambda width: pl.BlockSpec((None, tm, width), lambda b, i: (b, i, 0))
    tab = pl.BlockSpec((tm, LANES), lambda b, i: (i, 0))
    const = lambda shape: pl.BlockSpec(shape, lambda b, i: (0,) * len(shape))
    bf_out = jax.ShapeDtypeStruct((B, L, D_GROUP), BF16)
    f32_out = jax.ShapeDtypeStruct((B, L, D_GROUP), F32)
    return pl.pallas_call(
        _proj_prompt_kernel,
        grid=(B, nt),
        in_specs=[tok(D_MODEL), const((1, D_MODEL)), const((D_MODEL, 8 * D_GROUP)),
                  tab, tab, tab, tab, tab,
                  const((1, LANES)), const((1, LANES)), const((LANES, LANES))],
        out_specs=[tok(D_GROUP), tok(D_GROUP), tok(D_GROUP), tok(D_GROUP),
                   tok(D_GROUP), tok(D_GROUP), tok(D_GROUP),
                   pl.BlockSpec((None, tm * N_HEADS, D_HEAD), lambda b, i: (b, i, 0)),
                   pl.BlockSpec((None, None, N_HEADS, VT_ROWS, tm), lambda b, i: (b, i, 0, 0, 0)),
                   tok(D_GROUP)],
        out_shape=[bf_out, bf_out, bf_out, bf_out, bf_out, f32_out, bf_out,
                   jax.ShapeDtypeStruct((B, L * N_HEADS, D_HEAD), F32),
                   jax.ShapeDtypeStruct((B, nt, N_HEADS, VT_ROWS, tm), BF16), bf_out],
        compiler_params=pltpu.CompilerParams(
            dimension_semantics=("arbitrary", "arbitrary"), vmem_limit_bytes=VMEM_LIMIT_BYTES),
        name="proj_prompt",
    )(x, ng, w_bf, *tabs, gq, gk, seg)


def _ret_prompt_kernel(q_ref, k_ref, v_ref, g_ref, go_ref, u_ref, sfin_ref, s_scr, dm_scr):
    C = q_ref.shape[0]
    b, c = pl.program_id(0), pl.program_id(1)

    @pl.when((b == 0) & (c == 0))
    def _():
        rel = (lax.broadcasted_iota(jnp.int32, (C, C), 0) - lax.broadcasted_iota(jnp.int32, (C, C), 1)).astype(F32)
        for h in range(N_HEADS):
            dm_scr[h] = jnp.where(rel >= 0, jnp.exp(LOG_GAMMA[h] * jnp.maximum(rel, 0.0)), 0.0)

    @pl.when(c == 0)
    def _():
        s_scr[...] = jnp.zeros_like(s_scr)

    idx = lax.broadcasted_iota(jnp.int32, (C, D_HEAD), 0).astype(F32)
    for h in range(N_HEADS):
        lg = LOG_GAMMA[h]
        q, k, v = q_ref[:, _head(h)], k_ref[:, _head(h)], v_ref[:, _head(h)]
        s_old = s_scr[h]
        inner = lax.dot_general(q, k, _NT, preferred_element_type=F32) * dm_scr[h]
        q_dec = (q.astype(F32) * jnp.exp(lg * (idx + 1.0))).astype(BF16)
        o = (jnp.dot(inner.astype(BF16), v, preferred_element_type=F32)
             + jnp.dot(q_dec, s_old.astype(BF16), preferred_element_type=F32))
        k_dec = (k.astype(F32) * jnp.exp(lg * (C - 1.0 - idx))).astype(BF16)
        s_scr[h] = math.exp(lg * C) * s_old + lax.dot_general(k_dec, v, _TN, preferred_element_type=F32)
        u_ref[:, _head(h)] = (_rms_rows(o, go_ref[...]) * _silu(g_ref[:, _head(h)].astype(F32))).astype(BF16)

    @pl.when(c == pl.num_programs(1) - 1)
    def _():
        sfin_ref[...] = s_scr[...]


def _ret_prompt(rq, rk, rv, rg, go):
    B, L, _ = rq.shape
    C = RET_CHUNK
    tok = pl.BlockSpec((None, C, D_GROUP), lambda b, c: (b, c, 0))
    return pl.pallas_call(
        _ret_prompt_kernel,
        grid=(B, L // C),
        in_specs=[tok, tok, tok, tok, pl.BlockSpec((1, D_HEAD), lambda b, c: (0, 0))],
        out_specs=[tok, pl.BlockSpec((None, N_HEADS, D_HEAD, D_HEAD), lambda b, c: (b, 0, 0, 0))],
        out_shape=[jax.ShapeDtypeStruct((B, L, D_GROUP), BF16),
                   jax.ShapeDtypeStruct((B, N_HEADS, D_HEAD, D_HEAD), F32)],
        scratch_shapes=[pltpu.VMEM((N_HEADS, D_HEAD, D_HEAD), F32), pltpu.VMEM((N_HEADS, C, C), F32)],
        compiler_params=pltpu.CompilerParams(
            dimension_semantics=("arbitrary", "arbitrary"), vmem_limit_bytes=VMEM_LIMIT_BYTES),
        name="ret_prompt",
    )(rq, rk, rv, rg, go)


def _diff_prompt_kernel(q_ref, k_ref, vt_ref, g_ref, go_ref, lam_ref, u_ref,
                        acc_scr, m_scr, sa_scr, mta_scr, sb_scr, mtb_scr):
    tq = q_ref.shape[0]
    tk = vt_ref.shape[-1]
    n_half = tq // Q_CHUNK
    qi = pl.program_id(2)
    q = q_ref[...]
    lane = lax.broadcasted_iota(jnp.int32, q.shape, 1)
    zero = jnp.zeros_like(q)
    q_maps = (jnp.where(lane < D_MAP, q, zero), jnp.where(lane >= D_MAP, q, zero))
    chunks = [(mp, hf) for mp in range(2) for hf in range(n_half)]

    m_scr[...] = jnp.full(m_scr.shape, NEG_INF, F32)
    acc_scr[...] = jnp.zeros_like(acc_scr)

    def scores(j, s_buf, mt_buf):
        k = k_ref[pl.ds(pl.multiple_of(j * tk, tk), tk), :]
        for c, (mp, hf) in enumerate(chunks):
            qc = q_maps[mp][hf * Q_CHUNK:(hf + 1) * Q_CHUNK]
            s = lax.dot_general(k, qc, _NT, preferred_element_type=F32)
            s_buf[c] = s
            mt_buf[:, c * Q_CHUNK:(c + 1) * Q_CHUNK] = jnp.max(s, axis=0, keepdims=True)

    def consume(j, s_buf, mt_buf, diagonal):
        m_all, acc_all = m_scr[...], acc_scr[...]
        m_out, acc_out = [], []
        for c, (mp, hf) in enumerate(chunks):
            cols = slice(c * Q_CHUNK, (c + 1) * Q_CHUNK)
            m_old = m_all[:, cols]
            if diagonal:
                rows = (hf + 1) * Q_CHUNK
                s = s_buf[c, :rows, :]
                k_pos = lax.broadcasted_iota(jnp.int32, s.shape, 0)
                q_pos = hf * Q_CHUNK + lax.broadcasted_iota(jnp.int32, s.shape, 1)
                s = jnp.where(k_pos <= q_pos, s, NEG_INF)
                m_new = jnp.maximum(m_old, jnp.max(s, axis=0, keepdims=True))
            else:
                rows = tk
                s = s_buf[c]
                m_new = jnp.maximum(m_old, mt_buf[:, cols])
            alpha = jnp.exp2(m_old - m_new)
            p = jnp.exp2((s - m_new).astype(BF16))
            pv = jnp.dot(vt_ref[j, :, :rows], p, preferred_element_type=F32)
            acc_out.append(alpha * acc_all[:, cols] + pv)
            m_out.append(m_new)
        m_scr[...] = jnp.concatenate(m_out, axis=1)
        acc_scr[...] = jnp.concatenate(acc_out, axis=1)

    scores(0, sa_scr, mta_scr)

    def pair(i, carry):
        t = 2 * i
        scores(t + 1, sb_scr, mtb_scr)
        consume(t, sa_scr, mta_scr, diagonal=False)
        scores(t + 2, sa_scr, mta_scr)
        consume(t + 1, sb_scr, mtb_scr, diagonal=False)
        return carry

    lax.fori_loop(0, lax.shift_right_logical(qi, 1), pair, 0)
    odd = lax.rem(qi, 2) == 1

    @pl.when(odd)
    def _():
        scores(qi, sb_scr, mtb_scr)
        consume(qi - 1, sa_scr, mta_scr, diagonal=False)
        consume(qi, sb_scr, mtb_scr, diagonal=True)

    @pl.when(jnp.logical_not(odd))
    def _():
        consume(qi, sa_scr, mta_scr, diagonal=True)

    acc = acc_scr[...]
    o_all = acc[:D_HEAD] * (1.0 / acc[D_HEAD:D_HEAD + 1])
    o_t = o_all[:, :tq] - _lambda(lam_ref) * o_all[:, tq:]
    on = _rms_rows(o_t.T, go_ref[...]) * (1.0 - LAM_INIT)
    u_ref[...] = (on * _silu(g_ref[...].astype(F32))).astype(BF16)


def _diff_prompt(dq, dkb, dvt, dg, go, lam_rows):
    B, L, _ = dq.shape
    tq = TOKEN_TILE
    nk, tk = dvt.shape[1], dvt.shape[4]
    assert tq == tk and tq % Q_CHUNK == 0
    tok = pl.BlockSpec((None, tq, D_HEAD), lambda b, h, i: (b, i, h))
    return pl.pallas_call(
        _diff_prompt_kernel,
        grid=(B, N_HEADS, L // tq),
        in_specs=[tok,
                  pl.BlockSpec((None, L, D_HEAD), lambda b, h, i: (b, 0, h)),
                  pl.BlockSpec((None, nk, None, VT_ROWS, tk), lambda b, h, i: (b, 0, h, 0, 0)),
                  tok,
                  pl.BlockSpec((1, D_HEAD), lambda b, h, i: (0, 0)),
                  pl.BlockSpec((4, D_MAP), lambda b, h, i: (0, 0))],
        out_specs=tok,
        out_shape=jax.ShapeDtypeStruct((B, L, D_GROUP), BF16),
        scratch_shapes=[pltpu.VMEM((VT_ROWS, 2 * tq), F32), pltpu.VMEM((1, 2 * tq), F32)]
                       + [pltpu.VMEM((2 * tq // Q_CHUNK, tk, Q_CHUNK), F32), pltpu.VMEM((1, 2 * tq), F32)] * 2,
        compiler_params=pltpu.CompilerParams(
            dimension_semantics=("arbitrary", "arbitrary", "arbitrary"), vmem_limit_bytes=VMEM_LIMIT_BYTES),
        name="diff_prompt",
    )(dq, dkb, dvt, dg, go, lam_rows)


def _out_proj_kernel(x_ref, ur_ref, ud_ref, w_ref, y_ref):
    y_ref[...] = (x_ref[...]
                  + jnp.dot(ur_ref[...].astype(BF16), w_ref[:D_GROUP, :], preferred_element_type=F32)
                  + jnp.dot(ud_ref[...].astype(BF16), w_ref[D_GROUP:, :], preferred_element_type=F32))


def _out_proj(x2d, ur, ud, w_bf, tm):
    T = x2d.shape[0]
    tok = lambda width: pl.BlockSpec((tm, width), lambda i: (i, 0))
    return pl.pallas_call(
        _out_proj_kernel,
        grid=(T // tm,),
        in_specs=[tok(D_MODEL), tok(D_GROUP), tok(D_GROUP),
                  pl.BlockSpec((2 * D_GROUP, D_MODEL), lambda i: (0, 0))],
        out_specs=tok(D_MODEL),
        out_shape=jax.ShapeDtypeStruct((T, D_MODEL), F32),
        compiler_params=pltpu.CompilerParams(
            dimension_semantics=("arbitrary",), vmem_limit_bytes=VMEM_LIMIT_BYTES),
        name="out_proj",
    )(x2d, ur, ud, w_bf)


def _proj_sample_kernel(x_ref, ng_ref, w_ref, rc_ref, rs_ref, dc_ref, da_ref, db_ref, gq_ref, gk_ref, seg_ref,
                        rqt_ref, rkt_ref, dqt_ref, rv_ref, rg_ref, dq_ref, dk_ref, dv_ref, dg_ref):
    hb = _normed_input(x_ref, ng_ref)
    n = hb.shape[0]
    rc, rs = rc_ref[...], rs_ref[...]
    dc, da, db = dc_ref[...], da_ref[...], db_ref[...]
    seg = seg_ref[...]
    pad = jnp.zeros((LANES - n, D_HEAD), F32)

    def transposed(xh):
        return jnp.concatenate([xh, pad], axis=0).T

    z = _z_group(hb, w_ref, 0)
    for h in range(N_HEADS):
        rqt_ref[_head(h), :] = transposed(_rope_full(z[:, _head(h)], rc, rs))
    z = _z_group(hb, w_ref, 1)
    rck, rsk = rc * RET_K_SCALE, rs * RET_K_SCALE
    for h in range(N_HEADS):
        rkt_ref[_head(h), :] = transposed(_rope_full(z[:, _head(h)], rck, rsk))
    rv_ref[...] = _z_group(hb, w_ref, 2)
    rg_ref[...] = _z_group(hb, w_ref, 3)
    z = _z_group(hb, w_ref, 4)
    dcq, daq, dbq = dc * DIFF_Q_SCALE, da * DIFF_Q_SCALE, db * DIFF_Q_SCALE
    for h in range(N_HEADS):
        qh = _rope_partial(_map_rms(z[:, _head(h)], seg, gq_ref[...]), dcq, daq, dbq)
        dq_ref[:, _head(h)] = qh
        dqt_ref[_head(h), :] = transposed(qh)
    z = _z_group(hb, w_ref, 5)
    for h in range(N_HEADS):
        dk_ref[:, _head(h)] = _rope_partial(_map_rms(z[:, _head(h)], seg, gk_ref[...]), dc, da, db)
    dv_ref[...] = _z_group(hb, w_ref, 6)
    dg_ref[...] = _z_group(hb, w_ref, 7)


def _proj_sample(x, ng, w_bf, tabs, gq, gk, seg):
    n = x.shape[0]
    full = lambda shape: pl.BlockSpec(shape, lambda i: (0,) * len(shape))
    row_out = jax.ShapeDtypeStruct((n, D_GROUP), F32)
    col_out = jax.ShapeDtypeStruct((D_GROUP, LANES), F32)
    return pl.pallas_call(
        _proj_sample_kernel,
        grid=(1,),
        in_specs=[full((n, D_MODEL)), full((1, D_MODEL)), full((D_MODEL, 8 * D_GROUP))]
                 + [full((n, LANES))] * 5 + [full((1, LANES)), full((1, LANES)), full((LANES, LANES))],
        out_specs=[full((D_GROUP, LANES))] * 3 + [full((n, D_GROUP))] * 6,
        out_shape=[col_out] * 3 + [row_out] * 6,
        compiler_params=pltpu.CompilerParams(
            dimension_semantics=("arbitrary",), vmem_limit_bytes=VMEM_LIMIT_BYTES),
        name="proj_sample",
    )(x, ng, w_bf, *tabs, gq, gk, seg)


def _ret_sample_kernel(qt_ref, kt_ref, v_ref, g_ref, s_ref, go_ref, u_ref, snew_ref):
    b = pl.program_id(0)
    lane = lax.broadcasted_iota(jnp.int32, (D_HEAD, LANES), 1)

    def column(ref, h):
        return jnp.sum(jnp.where(lane == b, ref[_head(h), :], 0.0), axis=1, keepdims=True)

    for h in range(N_HEADS):
        gamma = math.exp(LOG_GAMMA[h])
        q_col, k_col = column(qt_ref, h), column(kt_ref, h)
        v_row = v_ref[:, _head(h)]
        s_old = s_ref[h]
        inner = jnp.sum(q_col * k_col, axis=0, keepdims=True)
        o = inner * v_row + jnp.sum((q_col * gamma) * s_old, axis=0, keepdims=True)
        snew_ref[h] = gamma * s_old + k_col * v_row
        u_ref[:, _head(h)] = _rms_rows(o, go_ref[...]) * _silu(g_ref[:, _head(h)])


def _ret_sample(rqt, rkt, rv3, rg3, state, go):
    n = state.shape[0]
    col = pl.BlockSpec((D_GROUP, LANES), lambda b: (0, 0))
    row = pl.BlockSpec((None, 1, D_GROUP), lambda b: (b, 0, 0))
    st = pl.BlockSpec((None, N_HEADS, D_HEAD, D_HEAD), lambda b: (b, 0, 0, 0))
    return pl.pallas_call(
        _ret_sample_kernel,
        grid=(n,),
        in_specs=[col, col, row, row, st, pl.BlockSpec((1, D_HEAD), lambda b: (0, 0))],
        out_specs=[row, st],
        out_shape=[jax.ShapeDtypeStruct((n, 1, D_GROUP), F32),
                   jax.ShapeDtypeStruct((n, N_HEADS, D_HEAD, D_HEAD), F32)],
        compiler_params=pltpu.CompilerParams(dimension_semantics=("arbitrary",)),
        name="ret_sample",
    )(rqt, rkt, rv3, rg3, state, go)


def _diff_sample_kernel(pt_ref, qt_ref, q_ref, kn_ref, vn_ref, g_ref, go_ref, lam_ref, *rest):
    del pt_ref
    pps = PAGES_PER_STEP
    k_refs, v_refs = rest[:pps], rest[pps:2 * pps]
    u_ref, qb_scr, acc_scr, m_scr, l_scr = rest[2 * pps:]
    b, s_idx = pl.program_id(0), pl.program_id(1)
    n_maps = 2 * N_HEADS

    @pl.when(s_idx == 0)
    def _():
        lane = lax.broadcasted_iota(jnp.int32, (D_GROUP, LANES), 1)
        q_col = jnp.sum(jnp.where(lane == b, qt_ref[...], 0.0), axis=1, keepdims=True)
        qb_scr[...] = jnp.broadcast_to(q_col, (D_GROUP, LANES))
        m_scr[...] = jnp.full(m_scr.shape, NEG_INF, F32)
        l_scr[...] = jnp.zeros_like(l_scr)
        acc_scr[...] = jnp.zeros_like(acc_scr)

    row_id = lax.broadcasted_iota(jnp.int32, (n_maps, PAGE_SIZE), 0)
    s_pages = []
    for i in range(pps):
        s_i = jnp.zeros((n_maps, PAGE_SIZE), F32)
        for r in range(n_maps):
            rows = slice(r * D_MAP, (r + 1) * D_MAP)
            s_r = jnp.sum(k_refs[i][rows, :] * qb_scr[rows, :], axis=0, keepdims=True)
            s_i = jnp.where(row_id == r, s_r, s_i)
        s_pages.append(s_i)
    s = jnp.concatenate(s_pages, axis=1)
    m_old = m_scr[...]
    m_new = jnp.maximum(m_old, jnp.max(s, axis=1, keepdims=True))
    alpha = jnp.exp2(m_old - m_new)
    p = jnp.exp2(s - m_new)
    l_scr[...] = alpha * l_scr[...] + jnp.sum(p, axis=1, keepdims=True)
    p_bf = p.astype(BF16)
    pv = []
    for h in range(N_HEADS):
        v_h = jnp.concatenate([v_refs[i][pl.ds(h, PAGE_SIZE, stride=N_HEADS), :] for i in range(pps)], axis=0)
        pv.append(jnp.dot(p_bf, v_h.astype(BF16), preferred_element_type=F32))
    acc_scr[...] = alpha * acc_scr[...] + jnp.concatenate(pv, axis=1)
    m_scr[...] = m_new

    @pl.when(s_idx == pl.num_programs(1) - 1)
    def _():
        r_idx = lax.broadcasted_iota(jnp.int32, (n_maps, D_GROUP), 0)
        c_idx = lax.broadcasted_iota(jnp.int32, (n_maps, D_GROUP), 1)
        in_map = (c_idx >= r_idx * D_MAP) & (c_idx < (r_idx + 1) * D_MAP)
        q_bd = jnp.where(in_map, jnp.broadcast_to(q_ref[...], (n_maps, D_GROUP)), 0.0)
        s_new = jnp.sum(q_bd * kn_ref[...], axis=1, keepdims=True)
        m_old = m_scr[...]
        m_fin = jnp.maximum(m_old, s_new)
        alpha = jnp.exp2(m_old - m_fin)
        p_new = jnp.exp2(s_new - m_fin)
        l_fin = alpha * l_scr[...] + p_new
        acc = (alpha * acc_scr[...] + p_new * vn_ref[...]) * (1.0 / l_fin)
        lam = _lambda(lam_ref)
        for h in range(N_HEADS):
            o = acc[2 * h:2 * h + 1, _head(h)] - lam * acc[2 * h + 1:2 * h + 2, _head(h)]
            on = _rms_rows(o, go_ref[...]) * (1.0 - LAM_INIT)
            u_ref[:, _head(h)] = on * _silu(g_ref[:, _head(h)])


def _diff_sample(page_table, dqt, dq3, dk3, dv3, dg3, k_pages, v_pages, go, lam_rows):
    n, n_pages = page_table.shape
    pps = PAGES_PER_STEP
    row = pl.BlockSpec((None, 1, D_GROUP), lambda b, s, pt: (b, 0, 0))

    def page(i):
        return pl.BlockSpec((None, D_GROUP, LANES), lambda b, s, pt: (pt[b, s * pps + i], 0, 0))

    n_maps = 2 * N_HEADS
    grid_spec = pltpu.PrefetchScalarGridSpec(
        num_scalar_prefetch=1,
        grid=(n, n_pages // pps),
        in_specs=[pl.BlockSpec((D_GROUP, LANES), lambda b, s, pt: (0, 0)), row, row, row, row,
                  pl.BlockSpec((1, D_HEAD), lambda b, s, pt: (0, 0)),
                  pl.BlockSpec((4, D_MAP), lambda b, s, pt: (0, 0))]
                 + [page(i) for i in range(pps)] + [page(i) for i in range(pps)],
        out_specs=row,
        scratch_shapes=[pltpu.VMEM((D_GROUP, LANES), F32), pltpu.VMEM((n_maps, D_GROUP), F32),
                        pltpu.VMEM((n_maps, 1), F32), pltpu.VMEM((n_maps, 1), F32)],
    )
    return pl.pallas_call(
        _diff_sample_kernel,
        grid_spec=grid_spec,
        out_shape=jax.ShapeDtypeStruct((n, 1, D_GROUP), F32),
        compiler_params=pltpu.CompilerParams(
            dimension_semantics=("arbitrary", "arbitrary"), vmem_limit_bytes=VMEM_LIMIT_BYTES),
        name="diff_sample",
    )(page_table, dqt, dq3, dk3, dv3, dg3, go, lam_rows, *([k_pages] * pps), *([v_pages] * pps))


def _rope_tables(pos):
    posf = pos.astype(F32)[:, None]
    fr = RET_ROPE_THETA ** (-jnp.arange(0, D_HEAD, 2, dtype=F32) / D_HEAD)
    ang = posf * fr[None, :]
    cos, sin = jnp.cos(ang), jnp.sin(ang)
    rc = jnp.concatenate([cos, cos], axis=-1)
    rs = jnp.concatenate([-sin, sin], axis=-1)
    half = N_ROT_DIFF // 2
    fd = ROPE_THETA ** (-jnp.arange(0, N_ROT_DIFF, 2, dtype=F32) / N_ROT_DIFF)
    angd = posf * fd[None, :]
    cd, sd = jnp.cos(angd), jnp.sin(angd)
    n = pos.shape[0]
    ones = jnp.ones((n, D_MAP - N_ROT_DIFF), F32)
    zeros = jnp.zeros((n, D_MAP - N_ROT_DIFF), F32)
    z8 = jnp.zeros((n, half), F32)
    dc = jnp.concatenate([cd, cd, ones], axis=-1)
    da = jnp.concatenate([-sd, z8, zeros], axis=-1)
    db = jnp.concatenate([z8, sd, zeros], axis=-1)
    twice = lambda t: jnp.concatenate([t, t], axis=-1)
    return rc, rs, twice(dc), twice(da), twice(db)


def kernel(x_prompt, x_sample, cache_k, cache_v, state_ret, page_table, norm_g, w_in, w_out,
           q_norm_g, k_norm_g, ret_out_g, diff_out_g, lambda_q1, lambda_k1, lambda_q2, lambda_k2):
    B, L, _ = x_prompt.shape
    n_s = x_sample.shape[0]
    n_pages = page_table.shape[1]
    past = n_pages * PAGE_SIZE
    l = 0

    ng = norm_g[l][None, :]
    w_in_bf = w_in[l].astype(BF16)
    w_out_bf = w_out[l].astype(BF16)
    gq = jnp.tile(q_norm_g[l], 2)[None, :]
    gk = jnp.tile(k_norm_g[l], 2)[None, :]
    go_ret = ret_out_g[l][None, :]
    go_diff = diff_out_g[l][None, :]
    lam_rows = jnp.stack([lambda_q1[l], lambda_k1[l], lambda_q2[l], lambda_k2[l]])
    lane = jnp.arange(LANES)
    seg = jnp.where((lane[:, None] // D_MAP) == (lane[None, :] // D_MAP), 1.0 / D_MAP, 0.0).astype(BF16)

    tabs_p = _rope_tables(jnp.arange(L))
    rq, rk, rv, rg, dq, dk, dkb, dv, dvt, dg = _proj_prompt(x_prompt, ng, w_in_bf, tabs_p, gq, gk, seg)
    u_ret, s_fin = _ret_prompt(rq, rk, rv, rg, go_ret)
    u_diff = _diff_prompt(dq, dkb, dvt, dg, go_diff, lam_rows)
    y_prompt = _out_proj(x_prompt.reshape(B * L, D_MODEL), u_ret.reshape(B * L, D_GROUP),
                         u_diff.reshape(B * L, D_GROUP), w_out_bf, TOKEN_TILE).reshape(B, L, D_MODEL)

    tabs_s = _rope_tables(jnp.full((n_s,), past, jnp.int32))
    rqt, rkt, dqt, rv_s, rg_s, dq_s, dk_s, dv_s, dg_s = _proj_sample(
        x_sample.reshape(n_s, D_MODEL), ng, w_in_bf, tabs_s, gq, gk, seg)
    as_rows = lambda t: t.reshape(n_s, 1, D_GROUP)
    u_ret_s, s_new = _ret_sample(rqt, rkt, as_rows(rv_s), as_rows(rg_s), state_ret[l], go_ret)
    n_phys = cache_k.shape[1]
    k_pages = jnp.transpose(cache_k[l], (0, 2, 3, 4, 1)).reshape(n_phys, D_GROUP, PAGE_SIZE)
    v_pages = cache_v[l].reshape(n_phys, PAGE_SIZE * N_HEADS, D_HEAD)
    u_diff_s = _diff_sample(page_table, dqt, as_rows(dq_s), as_rows(dk_s), as_rows(dv_s), as_rows(dg_s),
                            k_pages, v_pages, go_diff, lam_rows)
    y_sample = _out_proj(x_sample.reshape(n_s, D_MODEL), u_ret_s.reshape(n_s, D_GROUP),
                         u_diff_s.reshape(n_s, D_GROUP), w_out_bf, n_s).reshape(n_s, 1, D_MODEL)

    return (y_prompt, y_sample,
            dk.reshape(1, B, L, N_HEADS, 2, D_MAP), dv.reshape(1, B, L, N_HEADS, D_HEAD), s_fin[None],
            dk_s.reshape(1, n_s, 1, N_HEADS, 2, D_MAP), dv_s.reshape(1, n_s, 1, N_HEADS, D_HEAD), s_new[None])
```

```python
import functools
import math

import jax
import jax.numpy as jnp
from jax import lax
from jax.experimental import pallas as pl
from jax.experimental.pallas import tpu as pltpu

F32 = jnp.float32
BF16 = jnp.bfloat16

D_MODEL = 1024
N_HEADS = 4
D_HEAD = 128
D_GROUP = N_HEADS * D_HEAD
D_MAP = 64
N_ROT_DIFF = 16
ROPE_THETA = 500000.0
RET_ROPE_THETA = 10000.0
PAGE_SIZE = 128
EPS = 1e-6
NEG_INF = -1e30
LAM_INIT = 0.8 - 0.6 * math.exp(-0.3 * 0)
RET_K_SCALE = D_HEAD ** -0.5
DIFF_Q_SCALE = D_MAP ** -0.5 * math.log2(math.e)
LOG_GAMMA = tuple(math.log(1.0 - 2.0 ** (-5.0 - h)) for h in range(N_HEADS))

LANES = 128
VMEM_LIMIT_BYTES = 56 * 1024 * 1024

TOKEN_TILE = 512
Q_CHUNK = 256
ONES_ROWS = 16
VT_ROWS = D_HEAD + ONES_ROWS
RET_CHUNK = 256
PAGES_PER_GROUP = 4
DECODE_SLOTS = 3

_NT = (((1,), (1,)), ((), ()))
_TN = (((0,), (0,)), ((), ()))


def _head(h):
    return slice(h * D_HEAD, (h + 1) * D_HEAD)


def _silu(g):
    return g * (1.0 / (1.0 + jnp.exp(-g)))


def _rms_rows(x, gain):
    return x * lax.rsqrt(jnp.mean(x * x, axis=-1, keepdims=True) + EPS) * gain


def _rope_full(x, cos_t, sin_t):
    return x * cos_t + pltpu.roll(x, D_HEAD // 2, 1) * sin_t


def _rope_partial(x, c_t, a_t, b_t):
    half = N_ROT_DIFF // 2
    return x * c_t + pltpu.roll(x, LANES - half, 1) * a_t + pltpu.roll(x, half, 1) * b_t


def _map_rms(x, seg, gain):
    ms = jnp.dot((x * x).astype(BF16), seg, preferred_element_type=F32)
    return x * lax.rsqrt(ms + EPS) * gain


def _lambda(lam_ref):
    l = lam_ref[...]
    s1 = jnp.sum(l[0:1] * l[1:2], axis=-1, keepdims=True)
    s2 = jnp.sum(l[2:3] * l[3:4], axis=-1, keepdims=True)
    return jnp.exp(s1) - jnp.exp(s2) + LAM_INIT


def _normed_input(x_ref, ng_ref):
    x = x_ref[...]
    return _rms_rows(x, ng_ref[...]).astype(BF16)


def _z_group(hb, w_ref, g):
    return jnp.dot(hb, w_ref[:, g * D_GROUP:(g + 1) * D_GROUP], preferred_element_type=F32)


def _proj_prompt_kernel(x_ref, ng_ref, w_ref, rc_ref, rs_ref, dc_ref, da_ref, db_ref, gq_ref, gk_ref, seg_ref,
                        rq_ref, rk_ref, rv_ref, rg_ref, dq_ref, dk_ref, dkb_ref, dv_ref, dvt_ref, dg_ref):
    hb = _normed_input(x_ref, ng_ref)
    rc, rs = rc_ref[...], rs_ref[...]
    dc, da, db = dc_ref[...], da_ref[...], db_ref[...]
    seg = seg_ref[...]

    z = _z_group(hb, w_ref, 0)
    for h in range(N_HEADS):
        rq_ref[:, _head(h)] = _rope_full(z[:, _head(h)], rc, rs).astype(BF16)
    z = _z_group(hb, w_ref, 1)
    rck, rsk = rc * RET_K_SCALE, rs * RET_K_SCALE
    for h in range(N_HEADS):
        rk_ref[:, _head(h)] = _rope_full(z[:, _head(h)], rck, rsk).astype(BF16)
    rv_ref[...] = _z_group(hb, w_ref, 2).astype(BF16)
    rg_ref[...] = _z_group(hb, w_ref, 3).astype(BF16)

    z = _z_group(hb, w_ref, 4)
    dcq, daq, dbq = dc * DIFF_Q_SCALE, da * DIFF_Q_SCALE, db * DIFF_Q_SCALE
    for h in range(N_HEADS):
        qn = _map_rms(z[:, _head(h)], seg, gq_ref[...])
        dq_ref[:, _head(h)] = _rope_partial(qn, dcq, daq, dbq).astype(BF16)
    z = _z_group(hb, w_ref, 5)
    for h in range(N_HEADS):
        kn = _map_rms(z[:, _head(h)], seg, gk_ref[...])
        kr = _rope_partial(kn, dc, da, db)
        dk_ref[:, _head(h)] = kr
        dkb_ref[:, _head(h)] = kr.astype(BF16)
    z = _z_group(hb, w_ref, 6)
    tm = z.shape[0]
    ones = jnp.ones((ONES_ROWS, tm), BF16)
    for h in range(N_HEADS):
        zh = z[:, _head(h)]
        dv_ref[pl.ds(h, tm, stride=N_HEADS), :] = zh
        dvt_ref[h, :D_HEAD, :] = zh.T.astype(BF16)
        dvt_ref[h, D_HEAD:, :] = ones
    dg_ref[...] = _z_group(hb, w_ref, 7).astype(BF16)


def _proj_prompt(x, ng, w_bf, tabs, gq, gk, seg):
    B, L, _ = x.shape
    tm = TOKEN_TILE
    nt = L // tm
    tok = lambda width: pl.BlockSpec((None, tm, width), lambda b, i: (b, i, 0))
    tab = pl.BlockSpec((tm, LANES), lambda b, i: (i, 0))
    const = lambda shape: pl.BlockSpec(shape, lambda b, i: (0,) * len(shape))
    bf_out = jax.ShapeDtypeStruct((B, L, D_GROUP), BF16)
    f32_out = jax.ShapeDtypeStruct((B, L, D_GROUP), F32)
    return pl.pallas_call(
        _proj_prompt_kernel,
        grid=(B, nt),
        in_specs=[tok(D_MODEL), const((1, D_MODEL)), const((D_MODEL, 8 * D_GROUP)),
                  tab, tab, tab, tab, tab,
                  const((1, LANES)), const((1, LANES)), const((LANES, LANES))],
        out_specs=[tok(D_GROUP), tok(D_GROUP), tok(D_GROUP), tok(D_GROUP),
                   tok(D_GROUP), tok(D_GROUP), tok(D_GROUP),
                   pl.BlockSpec((None, tm * N_HEADS, D_HEAD), lambda b, i: (b, i, 0)),
                   pl.BlockSpec((None, None, N_HEADS, VT_ROWS, tm), lambda b, i: (b, i, 0, 0, 0)),
                   tok(D_GROUP)],
        out_shape=[bf_out, bf_out, bf_out, bf_out, bf_out, f32_out, bf_out,
                   jax.ShapeDtypeStruct((B, L * N_HEADS, D_HEAD), F32),
                   jax.ShapeDtypeStruct((B, nt, N_HEADS, VT_ROWS, tm), BF16), bf_out],
        compiler_params=pltpu.CompilerParams(
            dimension_semantics=("arbitrary", "arbitrary"), vmem_limit_bytes=VMEM_LIMIT_BYTES),
        name="proj_prompt",
    )(x, ng, w_bf, *tabs, gq, gk, seg)


def _ret_prompt_kernel(q_ref, k_ref, v_ref, g_ref, go_ref, u_ref, sfin_ref, s_scr, dm_scr):
    C = q_ref.shape[0]
    b, c = pl.program_id(0), pl.program_id(1)

    @pl.when((b == 0) & (c == 0))
    def _():
        rel = (lax.broadcasted_iota(jnp.int32, (C, C), 0) - lax.broadcasted_iota(jnp.int32, (C, C), 1)).astype(F32)
        for h in range(N_HEADS):
            dm_scr[h] = jnp.where(rel >= 0, jnp.exp(LOG_GAMMA[h] * jnp.maximum(rel, 0.0)), 0.0)

    @pl.when(c == 0)
    def _():
        s_scr[...] = jnp.zeros_like(s_scr)

    idx = lax.broadcasted_iota(jnp.int32, (C, D_HEAD), 0).astype(F32)
    for h in range(N_HEADS):
        lg = LOG_GAMMA[h]
        q, k, v = q_ref[:, _head(h)], k_ref[:, _head(h)], v_ref[:, _head(h)]
        s_old = s_scr[h]
        inner = lax.dot_general(q, k, _NT, preferred_element_type=F32) * dm_scr[h]
        q_dec = (q.astype(F32) * jnp.exp(lg * (idx + 1.0))).astype(BF16)
        o = (jnp.dot(inner.astype(BF16), v, preferred_element_type=F32)
             + jnp.dot(q_dec, s_old.astype(BF16), preferred_element_type=F32))
        k_dec = (k.astype(F32) * jnp.exp(lg * (C - 1.0 - idx))).astype(BF16)
        s_scr[h] = math.exp(lg * C) * s_old + lax.dot_general(k_dec, v, _TN, preferred_element_type=F32)
        u_ref[:, _head(h)] = (_rms_rows(o, go_ref[...]) * _silu(g_ref[:, _head(h)].astype(F32))).astype(BF16)

    @pl.when(c == pl.num_programs(1) - 1)
    def _():
        sfin_ref[...] = s_scr[...]


def _ret_prompt(rq, rk, rv, rg, go):
    B, L, _ = rq.shape
    C = RET_CHUNK
    tok = pl.BlockSpec((None, C, D_GROUP), lambda b, c: (b, c, 0))
    return pl.pallas_call(
        _ret_prompt_kernel,
        grid=(B, L // C),
        in_specs=[tok, tok, tok, tok, pl.BlockSpec((1, D_HEAD), lambda b, c: (0, 0))],
        out_specs=[tok, pl.BlockSpec((None, N_HEADS, D_HEAD, D_HEAD), lambda b, c: (b, 0, 0, 0))],
        out_shape=[jax.ShapeDtypeStruct((B, L, D_GROUP), BF16),
                   jax.ShapeDtypeStruct((B, N_HEADS, D_HEAD, D_HEAD), F32)],
        scratch_shapes=[pltpu.VMEM((N_HEADS, D_HEAD, D_HEAD), F32), pltpu.VMEM((N_HEADS, C, C), F32)],
        compiler_params=pltpu.CompilerParams(
            dimension_semantics=("arbitrary", "arbitrary"), vmem_limit_bytes=VMEM_LIMIT_BYTES),
        name="ret_prompt",
    )(rq, rk, rv, rg, go)


class _Decode:
    def __init__(self, pt_ref, qt_ref, q_ref, kn_ref, vn_ref, g_ref, go_ref, lam_ref, k_hbm, v_hbm, us_ref,
                 kbuf, vbuf, sem, qb_scr, acc_scr, m_scr, l_scr):
        self.__dict__.update(locals())
        n_seq, n_pages = pt_ref.shape
        self.groups_per_seq = n_pages // PAGES_PER_GROUP
        self.n_groups = n_seq * self.groups_per_seq
        self.n_maps = 2 * N_HEADS

    def _copies(self, gg):
        seq, first = gg // self.groups_per_seq, (gg % self.groups_per_seq) * PAGES_PER_GROUP
        slot = gg % DECODE_SLOTS
        out = []
        for i in range(PAGES_PER_GROUP):
            page = self.pt_ref[seq, first + i]
            out.append(pltpu.make_async_copy(self.k_hbm.at[page], self.kbuf.at[slot, i], self.sem.at[slot]))
            out.append(pltpu.make_async_copy(self.v_hbm.at[page], self.vbuf.at[slot, i], self.sem.at[slot]))
        return out

    def start(self, gg):
        for cp in self._copies(gg):
            cp.start()

    def wait(self, gg):
        for cp in self._copies(gg):
            cp.wait()

    def begin_sequence(self, seq):
        lane = lax.broadcasted_iota(jnp.int32, (D_GROUP, LANES), 1)
        q_col = jnp.sum(jnp.where(lane == seq, self.qt_ref[...], 0.0), axis=1, keepdims=True)
        self.qb_scr[...] = jnp.broadcast_to(q_col, (D_GROUP, LANES))
        self.m_scr[...] = jnp.full(self.m_scr.shape, NEG_INF, F32)
        self.l_scr[...] = jnp.zeros_like(self.l_scr)
        self.acc_scr[...] = jnp.zeros_like(self.acc_scr)

    def group(self, slot):
        row_id = lax.broadcasted_iota(jnp.int32, (self.n_maps, PAGE_SIZE), 0)
        s_pages = []
        for i in range(PAGES_PER_GROUP):
            s_i = jnp.zeros((self.n_maps, PAGE_SIZE), F32)
            for r in range(self.n_maps):
                rows = slice(r * D_MAP, (r + 1) * D_MAP)
                s_r = jnp.sum(self.kbuf[slot, i, rows, :] * self.qb_scr[rows, :], axis=0, keepdims=True)
                s_i = jnp.where(row_id == r, s_r, s_i)
            s_pages.append(s_i)
        s = jnp.concatenate(s_pages, axis=1)
        m_old = self.m_scr[...]
        m_new = jnp.maximum(m_old, jnp.max(s, axis=1, keepdims=True))
        alpha = jnp.exp2(m_old - m_new)
        p = jnp.exp2(s - m_new)
        self.l_scr[...] = alpha * self.l_scr[...] + jnp.sum(p, axis=1, keepdims=True)
        p_bf = p.astype(BF16)
        pv = []
        for h in range(N_HEADS):
            v_h = jnp.concatenate([self.vbuf[slot, i, pl.ds(h, PAGE_SIZE, stride=N_HEADS), :]
                                   for i in range(PAGES_PER_GROUP)], axis=0)
            pv.append(jnp.dot(p_bf, v_h.astype(BF16), preferred_element_type=F32))
        self.acc_scr[...] = alpha * self.acc_scr[...] + jnp.concatenate(pv, axis=1)
        self.m_scr[...] = m_new

    def end_sequence(self, seq):
        n_maps = self.n_maps
        r_idx = lax.broadcasted_iota(jnp.int32, (n_maps, D_GROUP), 0)
        c_idx = lax.broadcasted_iota(jnp.int32, (n_maps, D_GROUP), 1)
        in_map = (c_idx >= r_idx * D_MAP) & (c_idx < (r_idx + 1) * D_MAP)
        q_bd = jnp.where(in_map, jnp.broadcast_to(self.q_ref[seq], (n_maps, D_GROUP)), 0.0)
        s_new = jnp.sum(q_bd * self.kn_ref[seq], axis=1, keepdims=True)
        m_old = self.m_scr[...]
        m_fin = jnp.maximum(m_old, s_new)
        alpha = jnp.exp2(m_old - m_fin)
        p_new = jnp.exp2(s_new - m_fin)
        l_fin = alpha * self.l_scr[...] + p_new
        acc = (alpha * self.acc_scr[...] + p_new * self.vn_ref[seq]) * (1.0 / l_fin)
        lam = _lambda(self.lam_ref)
        gate = self.g_ref[seq]
        for h in range(N_HEADS):
            o = acc[2 * h:2 * h + 1, _head(h)] - lam * acc[2 * h + 1:2 * h + 2, _head(h)]
            on = _rms_rows(o, self.go_ref[...]) * (1.0 - LAM_INIT)
            self.us_ref[seq, :, _head(h)] = on * _silu(gate[:, _head(h)])


def _diff_prompt_kernel(tq_tab, tk_tab, pt_ref, q_ref, k_ref, vt_ref, g_ref, go_ref, lam_ref,
                        qts_ref, qs_ref, kns_ref, vns_ref, gs_ref, k_hbm, v_hbm, u_ref, us_ref,
                        acc_scr, m_scr, sa_scr, mta_scr, sb_scr, mtb_scr,
                        kbuf, vbuf, dsem, qb_scr, dacc_scr, dm_scr, dl_scr):
    tq = tk = vt_ref.shape[-1]
    n_half = tq // Q_CHUNK
    n_steps = tq_tab.shape[0] - 1
    chunks = [(mp, hf) for mp in range(2) for hf in range(n_half)]
    lane = lax.broadcasted_iota(jnp.int32, (Q_CHUNK, D_HEAD), 1)
    in_map = (lane < D_MAP, lane >= D_MAP)
    dec = _Decode(pt_ref, qts_ref, qs_ref, kns_ref, vns_ref, gs_ref, go_ref, lam_ref, k_hbm, v_hbm, us_ref,
                  kbuf, vbuf, dsem, qb_scr, dacc_scr, dm_scr, dl_scr)
    first_g = (pl.program_id(0) * pl.num_programs(1) + pl.program_id(1)) * n_steps

    m_scr[...] = jnp.full(m_scr.shape, NEG_INF, F32)
    acc_scr[...] = jnp.zeros_like(acc_scr)

    def scores(f, s_buf, mt_buf):
        q0 = pl.multiple_of(tq_tab[f] * tq, tq)
        k = k_ref[pl.ds(pl.multiple_of(tk_tab[f] * tk, tk), tk), :]
        for c, (mp, hf) in enumerate(chunks):
            qc = q_ref[pl.ds(q0 + hf * Q_CHUNK, Q_CHUNK), :]
            qc = jnp.where(in_map[mp], qc, jnp.zeros_like(qc))
            s = lax.dot_general(k, qc, _NT, preferred_element_type=F32)
            s_buf[c] = s
            mt_buf[:, c * Q_CHUNK:(c + 1) * Q_CHUNK] = jnp.max(s, axis=0, keepdims=True)

    def consume(f, s_buf, mt_buf, diagonal):
        j = tk_tab[f]
        m_all, acc_all = m_scr[...], acc_scr[...]
        m_out, acc_out = [], []
        for c, (mp, hf) in enumerate(chunks):
            cols = slice(c * Q_CHUNK, (c + 1) * Q_CHUNK)
            m_old = m_all[:, cols]
            if diagonal:
                rows = (hf + 1) * Q_CHUNK
                s = s_buf[c, :rows, :]
                k_pos = lax.broadcasted_iota(jnp.int32, s.shape, 0)
                q_pos = hf * Q_CHUNK + lax.broadcasted_iota(jnp.int32, s.shape, 1)
                s = jnp.where(k_pos <= q_pos, s, NEG_INF)
                m_new = jnp.maximum(m_old, jnp.max(s, axis=0, keepdims=True))
            else:
                rows = tk
                s = s_buf[c]
                m_new = jnp.maximum(m_old, mt_buf[:, cols])
            alpha = jnp.exp2(m_old - m_new)
            p = jnp.exp2((s - m_new).astype(BF16))
            pv = jnp.dot(vt_ref[j, :, :rows], p, preferred_element_type=F32)
            acc_out.append(alpha * acc_all[:, cols] + pv)
            m_out.append(m_new)
        if not diagonal:
            m_scr[...] = jnp.concatenate(m_out, axis=1)
            acc_scr[...] = jnp.concatenate(acc_out, axis=1)
            return
        acc = jnp.concatenate(acc_out, axis=1)
        o_all = acc[:D_HEAD] * (1.0 / acc[D_HEAD:D_HEAD + 1])
        o_t = o_all[:, :tq] - _lambda(lam_ref) * o_all[:, tq:]
        on = _rms_rows(o_t.T, go_ref[...]) * (1.0 - LAM_INIT)
        q_rows = pl.ds(pl.multiple_of(j * tq, tq), tq)
        u_ref[q_rows, :] = (on * _silu(g_ref[q_rows, :].astype(F32))).astype(BF16)
        m_scr[...] = jnp.full(m_scr.shape, NEG_INF, F32)
        acc_scr[...] = jnp.zeros_like(acc_scr)

    def step(f, cur, nxt):
        g = first_g + f
        seq, grp = g // dec.groups_per_seq, g % dec.groups_per_seq
        live = g < dec.n_groups

        @pl.when(g == 0)
        def _():
            dec.start(0)
            dec.start(1)

        @pl.when(g + 2 < dec.n_groups)
        def _():
            dec.start(g + 2)

        @pl.when(live)
        def _():
            dec.wait(g)

        @pl.when(live & (grp == 0))
        def _():
            dec.begin_sequence(seq)

        diagonal = tq_tab[f] == tk_tab[f]
        slot = g % DECODE_SLOTS

        @pl.when(diagonal)
        def _():
            scores(f + 1, *nxt)
            consume(f, *cur, diagonal=True)
            dec.group(slot)

        @pl.when(jnp.logical_not(diagonal))
        def _():
            scores(f + 1, *nxt)
            consume(f, *cur, diagonal=False)
            dec.group(slot)

        @pl.when(live & (grp == dec.groups_per_seq - 1))
        def _():
            dec.end_sequence(seq)

    buf_a, buf_b = (sa_scr, mta_scr), (sb_scr, mtb_scr)
    scores(0, *buf_a)

    def pair(i, carry):
        step(2 * i, buf_a, buf_b)
        step(2 * i + 1, buf_b, buf_a)
        return carry

    assert n_steps % 2 == 0
    lax.fori_loop(0, n_steps // 2, pair, 0)


def _diff_attention(dq, dkb, dvt, dg, go, lam_rows, page_table, dqt_s, dq_s, dk_s, dv_s, dg_s, k_pages, v_pages):
    B, L, _ = dq.shape
    nk, tk = dvt.shape[1], dvt.shape[4]
    n_s, n_pages = page_table.shape
    assert L == nk * tk and tk % Q_CHUNK == 0 and n_pages % PAGES_PER_GROUP == 0
    pairs = [(qi, j) for qi in range(nk) for j in range(qi + 1)] + [(0, 0)]
    n_steps = len(pairs) - 1
    assert B * N_HEADS * n_steps >= n_s * (n_pages // PAGES_PER_GROUP) >= DECODE_SLOTS
    tq_tab = jnp.asarray([p[0] for p in pairs], jnp.int32)
    tk_tab = jnp.asarray([p[1] for p in pairs], jnp.int32)
    seq = pl.BlockSpec((None, L, D_HEAD), lambda b, h, *_: (b, 0, h))
    whole = lambda shape: pl.BlockSpec(shape, lambda b, h, *_: (0,) * len(shape))
    rows_s = whole((n_s, 1, D_GROUP))
    hbm = pl.BlockSpec(memory_space=pl.ANY)
    n_maps = 2 * N_HEADS
    group_buf = pltpu.VMEM((DECODE_SLOTS, PAGES_PER_GROUP, D_GROUP, LANES), F32)
    grid_spec = pltpu.PrefetchScalarGridSpec(
        num_scalar_prefetch=3,
        grid=(B, N_HEADS),
        in_specs=[seq, seq,
                  pl.BlockSpec((None, nk, None, VT_ROWS, tk), lambda b, h, *_: (b, 0, h, 0, 0)),
                  seq, whole((1, D_HEAD)), whole((4, D_MAP)),
                  whole((D_GROUP, LANES)), rows_s, rows_s, rows_s, rows_s, hbm, hbm],
        out_specs=[seq, rows_s],
        scratch_shapes=[pltpu.VMEM((VT_ROWS, 2 * tk), F32), pltpu.VMEM((1, 2 * tk), F32)]
                       + [pltpu.VMEM((2 * tk // Q_CHUNK, tk, Q_CHUNK), F32), pltpu.VMEM((1, 2 * tk), F32)] * 2
                       + [group_buf, group_buf, pltpu.SemaphoreType.DMA((DECODE_SLOTS,)),
                          pltpu.VMEM((D_GROUP, LANES), F32), pltpu.VMEM((n_maps, D_GROUP), F32),
                          pltpu.VMEM((n_maps, 1), F32), pltpu.VMEM((n_maps, 1), F32)],
    )
    return pl.pallas_call(
        _diff_prompt_kernel,
        grid_spec=grid_spec,
        out_shape=[jax.ShapeDtypeStruct((B, L, D_GROUP), BF16), jax.ShapeDtypeStruct((n_s, 1, D_GROUP), F32)],
        compiler_params=pltpu.CompilerParams(
            dimension_semantics=("arbitrary", "arbitrary"), vmem_limit_bytes=VMEM_LIMIT_BYTES),
        name="diff_attention",
    )(tq_tab, tk_tab, page_table, dq, dkb, dvt, dg, go, lam_rows, dqt_s, dq_s, dk_s, dv_s, dg_s, k_pages, v_pages)


def _out_proj_kernel(x_ref, ur_ref, ud_ref, w_ref, y_ref):
    y_ref[...] = (x_ref[...]
                  + jnp.dot(ur_ref[...].astype(BF16), w_ref[:D_GROUP, :], preferred_element_type=F32)
                  + jnp.dot(ud_ref[...].astype(BF16), w_ref[D_GROUP:, :], preferred_element_type=F32))


def _out_proj(x2d, ur, ud, w_bf, tm):
    T = x2d.shape[0]
    tok = lambda width: pl.BlockSpec((tm, width), lambda i: (i, 0))
    return pl.pallas_call(
        _out_proj_kernel,
        grid=(T // tm,),
        in_specs=[tok(D_MODEL), tok(D_GROUP), tok(D_GROUP),
                  pl.BlockSpec((2 * D_GROUP, D_MODEL), lambda i: (0, 0))],
        out_specs=tok(D_MODEL),
        out_shape=jax.ShapeDtypeStruct((T, D_MODEL), F32),
        compiler_params=pltpu.CompilerParams(
            dimension_semantics=("arbitrary",), vmem_limit_bytes=VMEM_LIMIT_BYTES),
        name="out_proj",
    )(x2d, ur, ud, w_bf)


def _proj_sample_kernel(x_ref, ng_ref, w_ref, rc_ref, rs_ref, dc_ref, da_ref, db_ref, gq_ref, gk_ref, seg_ref,
                        rqt_ref, rkt_ref, dqt_ref, rv_ref, rg_ref, dq_ref, dk_ref, dv_ref, dg_ref):
    hb = _normed_input(x_ref, ng_ref)
    n = hb.shape[0]
    rc, rs = rc_ref[...], rs_ref[...]
    dc, da, db = dc_ref[...], da_ref[...], db_ref[...]
    seg = seg_ref[...]
    pad = jnp.zeros((LANES - n, D_HEAD), F32)

    def transposed(xh):
        return jnp.concatenate([xh, pad], axis=0).T

    z = _z_group(hb, w_ref, 0)
    for h in range(N_HEADS):
        rqt_ref[_head(h), :] = transposed(_rope_full(z[:, _head(h)], rc, rs))
    z = _z_group(hb, w_ref, 1)
    rck, rsk = rc * RET_K_SCALE, rs * RET_K_SCALE
    for h in range(N_HEADS):
        rkt_ref[_head(h), :] = transposed(_rope_full(z[:, _head(h)], rck, rsk))
    rv_ref[...] = _z_group(hb, w_ref, 2)
    rg_ref[...] = _z_group(hb, w_ref, 3)
    z = _z_group(hb, w_ref, 4)
    dcq, daq, dbq = dc * DIFF_Q_SCALE, da * DIFF_Q_SCALE, db * DIFF_Q_SCALE
    for h in range(N_HEADS):
        qh = _rope_partial(_map_rms(z[:, _head(h)], seg, gq_ref[...]), dcq, daq, dbq)
        dq_ref[:, _head(h)] = qh
        dqt_ref[_head(h), :] = transposed(qh)
    z = _z_group(hb, w_ref, 5)
    for h in range(N_HEADS):
        dk_ref[:, _head(h)] = _rope_partial(_map_rms(z[:, _head(h)], seg, gk_ref[...]), dc, da, db)
    dv_ref[...] = _z_group(hb, w_ref, 6)
    dg_ref[...] = _z_group(hb, w_ref, 7)


def _proj_sample(x, ng, w_bf, tabs, gq, gk, seg):
    n = x.shape[0]
    full = lambda shape: pl.BlockSpec(shape, lambda i: (0,) * len(shape))
    row_out = jax.ShapeDtypeStruct((n, D_GROUP), F32)
    col_out = jax.ShapeDtypeStruct((D_GROUP, LANES), F32)
    return pl.pallas_call(
        _proj_sample_kernel,
        grid=(1,),
        in_specs=[full((n, D_MODEL)), full((1, D_MODEL)), full((D_MODEL, 8 * D_GROUP))]
                 + [full((n, LANES))] * 5 + [full((1, LANES)), full((1, LANES)), full((LANES, LANES))],
        out_specs=[full((D_GROUP, LANES))] * 3 + [full((n, D_GROUP))] * 6,
        out_shape=[col_out] * 3 + [row_out] * 6,
        compiler_params=pltpu.CompilerParams(
            dimension_semantics=("arbitrary",), vmem_limit_bytes=VMEM_LIMIT_BYTES),
        name="proj_sample",
    )(x, ng, w_bf, *tabs, gq, gk, seg)


def _ret_sample_kernel(qt_ref, kt_ref, v_ref, g_ref, s_ref, go_ref, u_ref, snew_ref):
    b = pl.program_id(0)
    lane = lax.broadcasted_iota(jnp.int32, (D_HEAD, LANES), 1)

    def column(ref, h):
        return jnp.sum(jnp.where(lane == b, ref[_head(h), :], 0.0), axis=1, keepdims=True)

    for h in range(N_HEADS):
        gamma = math.exp(LOG_GAMMA[h])
        q_col, k_col = column(qt_ref, h), column(kt_ref, h)
        v_row = v_ref[:, _head(h)]
        s_old = s_ref[h]
        inner = jnp.sum(q_col * k_col, axis=0, keepdims=True)
        o = inner * v_row + jnp.sum((q_col * gamma) * s_old, axis=0, keepdims=True)
        snew_ref[h] = gamma * s_old + k_col * v_row
        u_ref[:, _head(h)] = _rms_rows(o, go_ref[...]) * _silu(g_ref[:, _head(h)])


def _ret_sample(rqt, rkt, rv3, rg3, state, go):
    n = state.shape[0]
    col = pl.BlockSpec((D_GROUP, LANES), lambda b: (0, 0))
    row = pl.BlockSpec((None, 1, D_GROUP), lambda b: (b, 0, 0))
    st = pl.BlockSpec((None, N_HEADS, D_HEAD, D_HEAD), lambda b: (b, 0, 0, 0))
    return pl.pallas_call(
        _ret_sample_kernel,
        grid=(n,),
        in_specs=[col, col, row, row, st, pl.BlockSpec((1, D_HEAD), lambda b: (0, 0))],
        out_specs=[row, st],
        out_shape=[jax.ShapeDtypeStruct((n, 1, D_GROUP), F32),
                   jax.ShapeDtypeStruct((n, N_HEADS, D_HEAD, D_HEAD), F32)],
        compiler_params=pltpu.CompilerParams(dimension_semantics=("arbitrary",)),
        name="ret_sample",
    )(rqt, rkt, rv3, rg3, state, go)


def _rope_tables(pos):
    posf = pos.astype(F32)[:, None]
    fr = RET_ROPE_THETA ** (-jnp.arange(0, D_HEAD, 2, dtype=F32) / D_HEAD)
    ang = posf * fr[None, :]
    cos, sin = jnp.cos(ang), jnp.sin(ang)
    rc = jnp.concatenate([cos, cos], axis=-1)
    rs = jnp.concatenate([-sin, sin], axis=-1)
    half = N_ROT_DIFF // 2
    fd = ROPE_THETA ** (-jnp.arange(0, N_ROT_DIFF, 2, dtype=F32) / N_ROT_DIFF)
    angd = posf * fd[None, :]
    cd, sd = jnp.cos(angd), jnp.sin(angd)
    n = pos.shape[0]
    ones = jnp.ones((n, D_MAP - N_ROT_DIFF), F32)
    zeros = jnp.zeros((n, D_MAP - N_ROT_DIFF), F32)
    z8 = jnp.zeros((n, half), F32)
    dc = jnp.concatenate([cd, cd, ones], axis=-1)
    da = jnp.concatenate([-sd, z8, zeros], axis=-1)
    db = jnp.concatenate([z8, sd, zeros], axis=-1)
    twice = lambda t: jnp.concatenate([t, t], axis=-1)
    return rc, rs, twice(dc), twice(da), twice(db)


def kernel(x_prompt, x_sample, cache_k, cache_v, state_ret, page_table, norm_g, w_in, w_out,
           q_norm_g, k_norm_g, ret_out_g, diff_out_g, lambda_q1, lambda_k1, lambda_q2, lambda_k2):
    B, L, _ = x_prompt.shape
    n_s = x_sample.shape[0]
    n_pages = page_table.shape[1]
    past = n_pages * PAGE_SIZE
    l = 0

    ng = norm_g[l][None, :]
    w_in_bf = w_in[l].astype(BF16)
    w_out_bf = w_out[l].astype(BF16)
    gq = jnp.tile(q_norm_g[l], 2)[None, :]
    gk = jnp.tile(k_norm_g[l], 2)[None, :]
    go_ret = ret_out_g[l][None, :]
    go_diff = diff_out_g[l][None, :]
    lam_rows = jnp.stack([lambda_q1[l], lambda_k1[l], lambda_q2[l], lambda_k2[l]])
    lane = jnp.arange(LANES)
    seg = jnp.where((lane[:, None] // D_MAP) == (lane[None, :] // D_MAP), 1.0 / D_MAP, 0.0).astype(BF16)

    tabs_p = _rope_tables(jnp.arange(L))
    rq, rk, rv, rg, dq, dk, dkb, dv, dvt, dg = _proj_prompt(x_prompt, ng, w_in_bf, tabs_p, gq, gk, seg)
    tabs_s = _rope_tables(jnp.full((n_s,), past, jnp.int32))
    rqt, rkt, dqt, rv_s, rg_s, dq_s, dk_s, dv_s, dg_s = _proj_sample(
        x_sample.reshape(n_s, D_MODEL), ng, w_in_bf, tabs_s, gq, gk, seg)
    as_rows = lambda t: t.reshape(n_s, 1, D_GROUP)

    u_ret, s_fin = _ret_prompt(rq, rk, rv, rg, go_ret)
    u_ret_s, s_new = _ret_sample(rqt, rkt, as_rows(rv_s), as_rows(rg_s), state_ret[l], go_ret)

    n_phys = cache_k.shape[1]
    k_pages = jnp.transpose(cache_k[l], (0, 2, 3, 4, 1)).reshape(n_phys, D_GROUP, PAGE_SIZE)
    v_pages = cache_v[l].reshape(n_phys, PAGE_SIZE * N_HEADS, D_HEAD)
    u_diff, u_diff_s = _diff_attention(dq, dkb, dvt, dg, go_diff, lam_rows, page_table, dqt, as_rows(dq_s),
                                       as_rows(dk_s), as_rows(dv_s), as_rows(dg_s), k_pages, v_pages)

    y_prompt = _out_proj(x_prompt.reshape(B * L, D_MODEL), u_ret.reshape(B * L, D_GROUP),
                         u_diff.reshape(B * L, D_GROUP), w_out_bf, TOKEN_TILE).reshape(B, L, D_MODEL)
    y_sample = _out_proj(x_sample.reshape(n_s, D_MODEL), u_ret_s.reshape(n_s, D_GROUP),
                         u_diff_s.reshape(n_s, D_GROUP), w_out_bf, n_s).reshape(n_s, 1, D_MODEL)

    return (y_prompt, y_sample,
            dk.reshape(1, B, L, N_HEADS, 2, D_MAP), dv.reshape(1, B, L, N_HEADS, D_HEAD), s_fin[None],
            dk_s.reshape(1, n_s, 1, N_HEADS, 2, D_MAP), dv_s.reshape(1, n_s, 1, N_HEADS, D_HEAD), s_new[None])
```

```python
import functools
import math

import jax
import jax.numpy as jnp
import numpy as np
from jax import lax
from jax.experimental import pallas as pl
from jax.experimental.pallas import tpu as pltpu

F32 = jnp.float32
BF16 = jnp.bfloat16

D_MODEL = 1024
N_HEADS = 4
D_HEAD = 128
D_GROUP = N_HEADS * D_HEAD
D_MAP = 64
N_ROT_DIFF = 16
ROPE_THETA = 500000.0
RET_ROPE_THETA = 10000.0
PAGE_SIZE = 128
EPS = 1e-6
NEG_INF = -1e30
LAM_INIT = 0.8 - 0.6 * math.exp(-0.3 * 0)
RET_K_SCALE = D_HEAD ** -0.5
DIFF_Q_SCALE = D_MAP ** -0.5 * math.log2(math.e)
LOG_GAMMA = tuple(math.log(1.0 - 2.0 ** (-5.0 - h)) for h in range(N_HEADS))

LANES = 128
VMEM_LIMIT_BYTES = 56 * 1024 * 1024

TOKEN_TILE = 512
Q_CHUNK = 256
ONES_ROWS = 16
VT_ROWS = D_HEAD + ONES_ROWS
RET_CHUNK = 256
RET_SAMPLE_SEQS = 8
PAGES_PER_GROUP = 4
DECODE_SLOTS = 3

_NT = (((1,), (1,)), ((), ()))
_TN = (((0,), (0,)), ((), ()))


def _head(h):
    return slice(h * D_HEAD, (h + 1) * D_HEAD)


def _silu(g):
    return g * (1.0 / (1.0 + jnp.exp(-g)))


def _rms_rows(x, gain):
    return x * lax.rsqrt(jnp.mean(x * x, axis=-1, keepdims=True) + EPS) * gain


def _rope_full(x, cos_t, sin_t):
    return x * cos_t + pltpu.roll(x, D_HEAD // 2, 1) * sin_t


def _rope_partial(x, c_t, a_t, b_t):
    half = N_ROT_DIFF // 2
    return x * c_t + pltpu.roll(x, LANES - half, 1) * a_t + pltpu.roll(x, half, 1) * b_t


def _map_rms(x, seg, gain):
    ms = jnp.dot((x * x).astype(BF16), seg, preferred_element_type=F32)
    return x * lax.rsqrt(ms + EPS) * gain


def _lambda(lam_ref):
    l = lam_ref[...]
    s1 = jnp.sum(l[0:1] * l[1:2], axis=-1, keepdims=True)
    s2 = jnp.sum(l[2:3] * l[3:4], axis=-1, keepdims=True)
    return jnp.exp(s1) - jnp.exp(s2) + LAM_INIT


def _normed_input(x_ref, ng_ref):
    x = x_ref[...]
    return _rms_rows(x, ng_ref[...]).astype(BF16)


def _z_group(hb, w_ref, g):
    return jnp.dot(hb, w_ref[:, g * D_GROUP:(g + 1) * D_GROUP], preferred_element_type=F32)


def _proj_prompt_kernel(x_ref, ng_ref, w_ref, rc_ref, rs_ref, dc_ref, da_ref, db_ref, gq_ref, gk_ref, seg_ref,
                        rq_ref, rk_ref, rv_ref, rg_ref, dq_ref, dk_ref, dkb_ref, dv_ref, dvt_ref, dg_ref):
    hb = _normed_input(x_ref, ng_ref)
    rc, rs = rc_ref[...], rs_ref[...]
    dc, da, db = dc_ref[...], da_ref[...], db_ref[...]
    seg = seg_ref[...]

    z = _z_group(hb, w_ref, 0)
    for h in range(N_HEADS):
        rq_ref[:, _head(h)] = _rope_full(z[:, _head(h)], rc, rs).astype(BF16)
    z = _z_group(hb, w_ref, 1)
    rck, rsk = rc * RET_K_SCALE, rs * RET_K_SCALE
    for h in range(N_HEADS):
        rk_ref[:, _head(h)] = _rope_full(z[:, _head(h)], rck, rsk).astype(BF16)
    rv_ref[...] = _z_group(hb, w_ref, 2).astype(BF16)
    rg_ref[...] = _z_group(hb, w_ref, 3).astype(BF16)

    z = _z_group(hb, w_ref, 4)
    dcq, daq, dbq = dc * DIFF_Q_SCALE, da * DIFF_Q_SCALE, db * DIFF_Q_SCALE
    for h in range(N_HEADS):
        qn = _map_rms(z[:, _head(h)], seg, gq_ref[...])
        dq_ref[:, _head(h)] = _rope_partial(qn, dcq, daq, dbq).astype(BF16)
    z = _z_group(hb, w_ref, 5)
    for h in range(N_HEADS):
        kn = _map_rms(z[:, _head(h)], seg, gk_ref[...])
        kr = _rope_partial(kn, dc, da, db)
        dk_ref[:, _head(h)] = kr
        dkb_ref[:, _head(h)] = kr.astype(BF16)
    z = _z_group(hb, w_ref, 6)
    tm = z.shape[0]
    ones = jnp.ones((ONES_ROWS, tm), BF16)
    for h in range(N_HEADS):
        zh = z[:, _head(h)]
        dv_ref[pl.ds(h, tm, stride=N_HEADS), :] = zh
        dvt_ref[h, :D_HEAD, :] = zh.T.astype(BF16)
        dvt_ref[h, D_HEAD:, :] = ones
    dg_ref[...] = _z_group(hb, w_ref, 7).astype(BF16)


def _proj_prompt(x, ng, w_bf, tabs, gq, gk, seg):
    B, L, _ = x.shape
    tm = TOKEN_TILE
    nt = L // tm
    tok = lambda width: pl.BlockSpec((None, tm, width), lambda b, i: (b, i, 0))
    tab = pl.BlockSpec((tm, LANES), lambda b, i: (i, 0))
    const = lambda shape: pl.BlockSpec(shape, lambda b, i: (0,) * len(shape))
    bf_out = jax.ShapeDtypeStruct((B, L, D_GROUP), BF16)
    f32_out = jax.ShapeDtypeStruct((B, L, D_GROUP), F32)
    return pl.pallas_call(
        _proj_prompt_kernel,
        grid=(B, nt),
        in_specs=[tok(D_MODEL), const((1, D_MODEL)), const((D_MODEL, 8 * D_GROUP)),
                  tab, tab, tab, tab, tab,
                  const((1, LANES)), const((1, LANES)), const((LANES, LANES))],
        out_specs=[tok(D_GROUP), tok(D_GROUP), tok(D_GROUP), tok(D_GROUP),
                   tok(D_GROUP), tok(D_GROUP), tok(D_GROUP),
                   pl.BlockSpec((None, tm * N_HEADS, D_HEAD), lambda b, i: (b, i, 0)),
                   pl.BlockSpec((None, None, N_HEADS, VT_ROWS, tm), lambda b, i: (b, i, 0, 0, 0)),
                   tok(D_GROUP)],
        out_shape=[bf_out, bf_out, bf_out, bf_out, bf_out, f32_out, bf_out,
                   jax.ShapeDtypeStruct((B, L * N_HEADS, D_HEAD), F32),
                   jax.ShapeDtypeStruct((B, nt, N_HEADS, VT_ROWS, tm), BF16), bf_out],
        compiler_params=pltpu.CompilerParams(
            dimension_semantics=("arbitrary", "arbitrary"), vmem_limit_bytes=VMEM_LIMIT_BYTES),
        name="proj_prompt",
    )(x, ng, w_bf, *tabs, gq, gk, seg)


def _ret_prompt_kernel(q_ref, k_ref, v_ref, g_ref, go_ref, u_ref, sfin_ref, s_scr, dm_scr, qd_scr, kd_scr):
    C = q_ref.shape[0]
    b, c = pl.program_id(0), pl.program_id(1)

    @pl.when((b == 0) & (c == 0))
    def _():
        rel = (lax.broadcasted_iota(jnp.int32, (C, C), 0) - lax.broadcasted_iota(jnp.int32, (C, C), 1)).astype(F32)
        idx = lax.broadcasted_iota(jnp.int32, (C, D_HEAD), 0).astype(F32)
        for h in range(N_HEADS):
            lg = LOG_GAMMA[h]
            dm_scr[h] = jnp.where(rel >= 0, jnp.exp(lg * jnp.maximum(rel, 0.0)), 0.0)
            qd_scr[h] = jnp.exp(lg * (idx + 1.0))
            kd_scr[h] = jnp.exp(lg * (C - 1.0 - idx))

    @pl.when(c == 0)
    def _():
        s_scr[...] = jnp.zeros_like(s_scr)

    for h in range(N_HEADS):
        lg = LOG_GAMMA[h]
        q, k, v = q_ref[:, _head(h)], k_ref[:, _head(h)], v_ref[:, _head(h)]
        s_old = s_scr[h]
        inner = lax.dot_general(q, k, _NT, preferred_element_type=F32) * dm_scr[h]
        q_dec = (q.astype(F32) * qd_scr[h]).astype(BF16)
        o = (jnp.dot(inner.astype(BF16), v, preferred_element_type=F32)
             + jnp.dot(q_dec, s_old.astype(BF16), preferred_element_type=F32))
        k_dec = (k.astype(F32) * kd_scr[h]).astype(BF16)
        s_scr[h] = math.exp(lg * C) * s_old + lax.dot_general(k_dec, v, _TN, preferred_element_type=F32)
        u_ref[:, _head(h)] = (_rms_rows(o, go_ref[...]) * _silu(g_ref[:, _head(h)].astype(F32))).astype(BF16)

    @pl.when(c == pl.num_programs(1) - 1)
    def _():
        sfin_ref[...] = s_scr[...]


def _ret_prompt(rq, rk, rv, rg, go):
    B, L, _ = rq.shape
    C = RET_CHUNK
    tok = pl.BlockSpec((None, C, D_GROUP), lambda b, c: (b, c, 0))
    return pl.pallas_call(
        _ret_prompt_kernel,
        grid=(B, L // C),
        in_specs=[tok, tok, tok, tok, pl.BlockSpec((1, D_HEAD), lambda b, c: (0, 0))],
        out_specs=[tok, pl.BlockSpec((None, N_HEADS, D_HEAD, D_HEAD), lambda b, c: (b, 0, 0, 0))],
        out_shape=[jax.ShapeDtypeStruct((B, L, D_GROUP), BF16),
                   jax.ShapeDtypeStruct((B, N_HEADS, D_HEAD, D_HEAD), F32)],
        scratch_shapes=[pltpu.VMEM((N_HEADS, D_HEAD, D_HEAD), F32), pltpu.VMEM((N_HEADS, C, C), F32),
                        pltpu.VMEM((N_HEADS, C, D_HEAD), F32), pltpu.VMEM((N_HEADS, C, D_HEAD), F32)],
        compiler_params=pltpu.CompilerParams(
            dimension_semantics=("arbitrary", "arbitrary"), vmem_limit_bytes=VMEM_LIMIT_BYTES),
        name="ret_prompt",
    )(rq, rk, rv, rg, go)


class _Decode:
    def __init__(self, pt_ref, qt_ref, q_ref, kn_ref, vn_ref, g_ref, go_ref, lam_ref, k_hbm, v_hbm, us_ref,
                 kbuf, vbuf, sem, qb_scr, acc_scr, m_scr, l_scr):
        self.__dict__.update(locals())
        n_seq, n_pages = pt_ref.shape
        self.groups_per_seq = n_pages // PAGES_PER_GROUP
        self.n_groups = n_seq * self.groups_per_seq
        self.n_maps = 2 * N_HEADS

    def _copies(self, gg):
        seq, first = gg // self.groups_per_seq, (gg % self.groups_per_seq) * PAGES_PER_GROUP
        slot = gg % DECODE_SLOTS
        out = []
        for i in range(PAGES_PER_GROUP):
            page = self.pt_ref[seq, first + i]
            out.append(pltpu.make_async_copy(self.k_hbm.at[page], self.kbuf.at[slot, i], self.sem.at[slot]))
            out.append(pltpu.make_async_copy(self.v_hbm.at[page], self.vbuf.at[slot, i], self.sem.at[slot]))
        return out

    def start(self, gg):
        for cp in self._copies(gg):
            cp.start()

    def wait(self, gg):
        for cp in self._copies(gg):
            cp.wait()

    def begin_sequence(self, seq):
        lane = lax.broadcasted_iota(jnp.int32, (D_GROUP, LANES), 1)
        q_col = jnp.sum(jnp.where(lane == seq, self.qt_ref[...], 0.0), axis=1, keepdims=True)
        self.qb_scr[...] = jnp.broadcast_to(q_col, (D_GROUP, LANES))
        self.m_scr[...] = jnp.full(self.m_scr.shape, NEG_INF, F32)
        self.l_scr[...] = jnp.zeros_like(self.l_scr)
        self.acc_scr[...] = jnp.zeros_like(self.acc_scr)

    def group(self, slot):
        row_id = lax.broadcasted_iota(jnp.int32, (self.n_maps, PAGE_SIZE), 0)
        s_pages = []
        for i in range(PAGES_PER_GROUP):
            s_i = jnp.zeros((self.n_maps, PAGE_SIZE), F32)
            for r in range(self.n_maps):
                rows = slice(r * D_MAP, (r + 1) * D_MAP)
                s_r = jnp.sum(self.kbuf[slot, i, rows, :] * self.qb_scr[rows, :], axis=0, keepdims=True)
                s_i = jnp.where(row_id == r, s_r, s_i)
            s_pages.append(s_i)
        s = jnp.concatenate(s_pages, axis=1)
        m_old = self.m_scr[...]
        m_new = jnp.maximum(m_old, jnp.max(s, axis=1, keepdims=True))
        alpha = jnp.exp2(m_old - m_new)
        p = jnp.exp2(s - m_new)
        self.l_scr[...] = alpha * self.l_scr[...] + jnp.sum(p, axis=1, keepdims=True)
        p_bf = p.astype(BF16)
        pv = []
        for h in range(N_HEADS):
            v_h = jnp.concatenate([self.vbuf[slot, i, pl.ds(h, PAGE_SIZE, stride=N_HEADS), :]
                                   for i in range(PAGES_PER_GROUP)], axis=0)
            pv.append(jnp.dot(p_bf, v_h.astype(BF16), preferred_element_type=F32))
        self.acc_scr[...] = alpha * self.acc_scr[...] + jnp.concatenate(pv, axis=1)
        self.m_scr[...] = m_new

    def end_sequence(self, seq):
        n_maps = self.n_maps
        r_idx = lax.broadcasted_iota(jnp.int32, (n_maps, D_GROUP), 0)
        c_idx = lax.broadcasted_iota(jnp.int32, (n_maps, D_GROUP), 1)
        in_map = (c_idx >= r_idx * D_MAP) & (c_idx < (r_idx + 1) * D_MAP)
        q_bd = jnp.where(in_map, jnp.broadcast_to(self.q_ref[seq], (n_maps, D_GROUP)), 0.0)
        s_new = jnp.sum(q_bd * self.kn_ref[seq], axis=1, keepdims=True)
        m_old = self.m_scr[...]
        m_fin = jnp.maximum(m_old, s_new)
        alpha = jnp.exp2(m_old - m_fin)
        p_new = jnp.exp2(s_new - m_fin)
        l_fin = alpha * self.l_scr[...] + p_new
        acc = (alpha * self.acc_scr[...] + p_new * self.vn_ref[seq]) * (1.0 / l_fin)
        lam = _lambda(self.lam_ref)
        gate = self.g_ref[seq]
        for h in range(N_HEADS):
            o = acc[2 * h:2 * h + 1, _head(h)] - lam * acc[2 * h + 1:2 * h + 2, _head(h)]
            on = _rms_rows(o, self.go_ref[...]) * (1.0 - LAM_INIT)
            self.us_ref[seq, :, _head(h)] = on * _silu(gate[:, _head(h)])


def _diff_prompt_kernel(tq_tab, tk_tab, pt_ref, q_ref, k_ref, vt_ref, g_ref, go_ref, lam_ref,
                        qts_ref, qs_ref, kns_ref, vns_ref, gs_ref, k_hbm, v_hbm, u_ref, us_ref,
                        acc_scr, m_scr, sa_scr, mta_scr, sb_scr, mtb_scr,
                        kbuf, vbuf, dsem, qb_scr, dacc_scr, dm_scr, dl_scr):
    tq = tk = vt_ref.shape[-1]
    n_half = tq // Q_CHUNK
    n_steps = tq_tab.shape[0] - 1
    chunks = [(mp, hf) for mp in range(2) for hf in range(n_half)]
    lane = lax.broadcasted_iota(jnp.int32, (Q_CHUNK, D_HEAD), 1)
    in_map = (lane < D_MAP, lane >= D_MAP)
    dec = _Decode(pt_ref, qts_ref, qs_ref, kns_ref, vns_ref, gs_ref, go_ref, lam_ref, k_hbm, v_hbm, us_ref,
                  kbuf, vbuf, dsem, qb_scr, dacc_scr, dm_scr, dl_scr)
    first_g = (pl.program_id(0) * pl.num_programs(1) + pl.program_id(1)) * n_steps

    m_scr[...] = jnp.full(m_scr.shape, NEG_INF, F32)
    acc_scr[...] = jnp.zeros_like(acc_scr)

    def scores(f, s_buf, mt_buf):
        q0 = pl.multiple_of(tq_tab[f] * tq, tq)
        k = k_ref[pl.ds(pl.multiple_of(tk_tab[f] * tk, tk), tk), :]
        for c, (mp, hf) in enumerate(chunks):
            qc = q_ref[pl.ds(q0 + hf * Q_CHUNK, Q_CHUNK), :]
            qc = jnp.where(in_map[mp], qc, jnp.zeros_like(qc))
            s = lax.dot_general(k, qc, _NT, preferred_element_type=F32)
            s_buf[c] = s
            mt_buf[:, c * Q_CHUNK:(c + 1) * Q_CHUNK] = jnp.max(s, axis=0, keepdims=True)

    def consume(f, s_buf, mt_buf, diagonal):
        j = tk_tab[f]
        m_all, acc_all = m_scr[...], acc_scr[...]
        m_out, acc_out = [], []
        for c, (mp, hf) in enumerate(chunks):
            cols = slice(c * Q_CHUNK, (c + 1) * Q_CHUNK)
            m_old = m_all[:, cols]
            if diagonal:
                rows = (hf + 1) * Q_CHUNK
                s = s_buf[c, :rows, :]
                k_pos = lax.broadcasted_iota(jnp.int32, s.shape, 0)
                q_pos = hf * Q_CHUNK + lax.broadcasted_iota(jnp.int32, s.shape, 1)
                s = jnp.where(k_pos <= q_pos, s, NEG_INF)
                m_new = jnp.maximum(m_old, jnp.max(s, axis=0, keepdims=True))
            else:
                rows = tk
                s = s_buf[c]
                m_new = jnp.maximum(m_old, mt_buf[:, cols])
            alpha = jnp.exp2(m_old - m_new)
            p = jnp.exp2((s - m_new).astype(BF16))
            pv = jnp.dot(vt_ref[j, :, :rows], p, preferred_element_type=F32)
            acc_out.append(alpha * acc_all[:, cols] + pv)
            m_out.append(m_new)
        if not diagonal:
            m_scr[...] = jnp.concatenate(m_out, axis=1)
            acc_scr[...] = jnp.concatenate(acc_out, axis=1)
            return
        acc = jnp.concatenate(acc_out, axis=1)
        o_all = acc[:D_HEAD] * (1.0 / acc[D_HEAD:D_HEAD + 1])
        o_t = o_all[:, :tq] - _lambda(lam_ref) * o_all[:, tq:]
        on = _rms_rows(o_t.T, go_ref[...]) * (1.0 - LAM_INIT)
        q_rows = pl.ds(pl.multiple_of(j * tq, tq), tq)
        u_ref[q_rows, :] = (on * _silu(g_ref[q_rows, :].astype(F32))).astype(BF16)
        m_scr[...] = jnp.full(m_scr.shape, NEG_INF, F32)
        acc_scr[...] = jnp.zeros_like(acc_scr)

    def step(f, cur, nxt):
        g = first_g + f
        seq, grp = g // dec.groups_per_seq, g % dec.groups_per_seq
        live = g < dec.n_groups

        @pl.when(g == 0)
        def _():
            dec.start(0)
            dec.start(1)

        @pl.when(g + 2 < dec.n_groups)
        def _():
            dec.start(g + 2)

        @pl.when(live)
        def _():
            dec.wait(g)

        @pl.when(live & (grp == 0))
        def _():
            dec.begin_sequence(seq)

        diagonal = tq_tab[f] == tk_tab[f]
        slot = g % DECODE_SLOTS

        @pl.when(diagonal)
        def _():
            scores(f + 1, *nxt)
            consume(f, *cur, diagonal=True)
            dec.group(slot)

        @pl.when(jnp.logical_not(diagonal))
        def _():
            scores(f + 1, *nxt)
            consume(f, *cur, diagonal=False)
            dec.group(slot)

        @pl.when(live & (grp == dec.groups_per_seq - 1))
        def _():
            dec.end_sequence(seq)

    buf_a, buf_b = (sa_scr, mta_scr), (sb_scr, mtb_scr)
    scores(0, *buf_a)

    def pair(i, carry):
        step(2 * i, buf_a, buf_b)
        step(2 * i + 1, buf_b, buf_a)
        return carry

    assert n_steps % 2 == 0
    lax.fori_loop(0, n_steps // 2, pair, 0)


def _diff_attention(dq, dkb, dvt, dg, go, lam_rows, page_table, dqt_s, dq_s, dk_s, dv_s, dg_s, k_pages, v_pages):
    B, L, _ = dq.shape
    nk, tk = dvt.shape[1], dvt.shape[4]
    n_s, n_pages = page_table.shape
    assert L == nk * tk and tk % Q_CHUNK == 0 and n_pages % PAGES_PER_GROUP == 0
    pairs = [(qi, j) for qi in range(nk) for j in range(qi + 1)] + [(0, 0)]
    n_steps = len(pairs) - 1
    assert B * N_HEADS * n_steps >= n_s * (n_pages // PAGES_PER_GROUP) >= DECODE_SLOTS
    tq_tab = jnp.asarray([p[0] for p in pairs], jnp.int32)
    tk_tab = jnp.asarray([p[1] for p in pairs], jnp.int32)
    seq = pl.BlockSpec((None, L, D_HEAD), lambda b, h, *_: (b, 0, h))
    whole = lambda shape: pl.BlockSpec(shape, lambda b, h, *_: (0,) * len(shape))
    rows_s = whole((n_s, 1, D_GROUP))
    hbm = pl.BlockSpec(memory_space=pl.ANY)
    n_maps = 2 * N_HEADS
    group_buf = pltpu.VMEM((DECODE_SLOTS, PAGES_PER_GROUP, D_GROUP, LANES), F32)
    grid_spec = pltpu.PrefetchScalarGridSpec(
        num_scalar_prefetch=3,
        grid=(B, N_HEADS),
        in_specs=[seq, seq,
                  pl.BlockSpec((None, nk, None, VT_ROWS, tk), lambda b, h, *_: (b, 0, h, 0, 0)),
                  seq, whole((1, D_HEAD)), whole((4, D_MAP)),
                  whole((D_GROUP, LANES)), rows_s, rows_s, rows_s, rows_s, hbm, hbm],
        out_specs=[seq, rows_s],
        scratch_shapes=[pltpu.VMEM((VT_ROWS, 2 * tk), F32), pltpu.VMEM((1, 2 * tk), F32)]
                       + [pltpu.VMEM((2 * tk // Q_CHUNK, tk, Q_CHUNK), F32), pltpu.VMEM((1, 2 * tk), F32)] * 2
                       + [group_buf, group_buf, pltpu.SemaphoreType.DMA((DECODE_SLOTS,)),
                          pltpu.VMEM((D_GROUP, LANES), F32), pltpu.VMEM((n_maps, D_GROUP), F32),
                          pltpu.VMEM((n_maps, 1), F32), pltpu.VMEM((n_maps, 1), F32)],
    )
    return pl.pallas_call(
        _diff_prompt_kernel,
        grid_spec=grid_spec,
        out_shape=[jax.ShapeDtypeStruct((B, L, D_GROUP), BF16), jax.ShapeDtypeStruct((n_s, 1, D_GROUP), F32)],
        compiler_params=pltpu.CompilerParams(
            dimension_semantics=("arbitrary", "arbitrary"), vmem_limit_bytes=VMEM_LIMIT_BYTES),
        name="diff_attention",
    )(tq_tab, tk_tab, page_table, dq, dkb, dvt, dg, go, lam_rows, dqt_s, dq_s, dk_s, dv_s, dg_s, k_pages, v_pages)


def _out_proj_kernel(x_ref, ur_ref, ud_ref, w_ref, y_ref):
    y_ref[...] = (x_ref[...]
                  + jnp.dot(ur_ref[...].astype(BF16), w_ref[:D_GROUP, :], preferred_element_type=F32)
                  + jnp.dot(ud_ref[...].astype(BF16), w_ref[D_GROUP:, :], preferred_element_type=F32))


def _out_proj(x2d, ur, ud, w_bf, tm):
    T = x2d.shape[0]
    tok = lambda width: pl.BlockSpec((tm, width), lambda i: (i, 0))
    return pl.pallas_call(
        _out_proj_kernel,
        grid=(T // tm,),
        in_specs=[tok(D_MODEL), tok(D_GROUP), tok(D_GROUP),
                  pl.BlockSpec((2 * D_GROUP, D_MODEL), lambda i: (0, 0))],
        out_specs=tok(D_MODEL),
        out_shape=jax.ShapeDtypeStruct((T, D_MODEL), F32),
        compiler_params=pltpu.CompilerParams(
            dimension_semantics=("arbitrary",), vmem_limit_bytes=VMEM_LIMIT_BYTES),
        name="out_proj",
    )(x2d, ur, ud, w_bf)


def _proj_sample_kernel(x_ref, ng_ref, w_ref, rc_ref, rs_ref, dc_ref, da_ref, db_ref, gq_ref, gk_ref, seg_ref,
                        rqt_ref, rkt_ref, dqt_ref, rv_ref, rg_ref, dq_ref, dk_ref, dv_ref, dg_ref):
    hb = _normed_input(x_ref, ng_ref)
    n = hb.shape[0]
    rc, rs = rc_ref[...], rs_ref[...]
    dc, da, db = dc_ref[...], da_ref[...], db_ref[...]
    seg = seg_ref[...]
    pad = jnp.zeros((LANES - n, D_HEAD), F32)

    def transposed(xh):
        return jnp.concatenate([xh, pad], axis=0).T

    z = _z_group(hb, w_ref, 0)
    for h in range(N_HEADS):
        rqt_ref[_head(h), :] = transposed(_rope_full(z[:, _head(h)], rc, rs))
    z = _z_group(hb, w_ref, 1)
    rck, rsk = rc * RET_K_SCALE, rs * RET_K_SCALE
    for h in range(N_HEADS):
        rkt_ref[_head(h), :] = transposed(_rope_full(z[:, _head(h)], rck, rsk))
    rv_ref[...] = _z_group(hb, w_ref, 2)
    rg_ref[...] = _z_group(hb, w_ref, 3)
    z = _z_group(hb, w_ref, 4)
    dcq, daq, dbq = dc * DIFF_Q_SCALE, da * DIFF_Q_SCALE, db * DIFF_Q_SCALE
    for h in range(N_HEADS):
        qh = _rope_partial(_map_rms(z[:, _head(h)], seg, gq_ref[...]), dcq, daq, dbq)
        dq_ref[:, _head(h)] = qh
        dqt_ref[_head(h), :] = transposed(qh)
    z = _z_group(hb, w_ref, 5)
    for h in range(N_HEADS):
        dk_ref[:, _head(h)] = _rope_partial(_map_rms(z[:, _head(h)], seg, gk_ref[...]), dc, da, db)
    dv_ref[...] = _z_group(hb, w_ref, 6)
    dg_ref[...] = _z_group(hb, w_ref, 7)


def _proj_sample(x, ng, w_bf, tabs, gq, gk, seg):
    n = x.shape[0]
    full = lambda shape: pl.BlockSpec(shape, lambda i: (0,) * len(shape))
    row_out = jax.ShapeDtypeStruct((n, D_GROUP), F32)
    col_out = jax.ShapeDtypeStruct((D_GROUP, LANES), F32)
    return pl.pallas_call(
        _proj_sample_kernel,
        grid=(1,),
        in_specs=[full((n, D_MODEL)), full((1, D_MODEL)), full((D_MODEL, 8 * D_GROUP))]
                 + [full((n, LANES))] * 5 + [full((1, LANES)), full((1, LANES)), full((LANES, LANES))],
        out_specs=[full((D_GROUP, LANES))] * 3 + [full((n, D_GROUP))] * 6,
        out_shape=[col_out] * 3 + [row_out] * 6,
        compiler_params=pltpu.CompilerParams(
            dimension_semantics=("arbitrary",), vmem_limit_bytes=VMEM_LIMIT_BYTES),
        name="proj_sample",
    )(x, ng, w_bf, *tabs, gq, gk, seg)


def _ret_sample_kernel(qt_ref, kt_ref, v_ref, g_ref, s_ref, go_ref, u_ref, snew_ref):
    per_step = v_ref.shape[0]
    lane = lax.broadcasted_iota(jnp.int32, (D_HEAD, LANES), 1)

    def column(ref, h, b):
        return jnp.sum(jnp.where(lane == b, ref[_head(h), :], 0.0), axis=1, keepdims=True)

    for i in range(per_step):
        b = pl.program_id(0) * per_step + i
        for h in range(N_HEADS):
            gamma = math.exp(LOG_GAMMA[h])
            q_col, k_col = column(qt_ref, h, b), column(kt_ref, h, b)
            v_row = v_ref[i, :, _head(h)]
            s_old = s_ref[i, h]
            inner = jnp.sum(q_col * k_col, axis=0, keepdims=True)
            o = inner * v_row + jnp.sum((q_col * gamma) * s_old, axis=0, keepdims=True)
            snew_ref[i, h] = gamma * s_old + k_col * v_row
            u_ref[i, :, _head(h)] = _rms_rows(o, go_ref[...]) * _silu(g_ref[i, :, _head(h)])


def _ret_sample(rqt, rkt, rv3, rg3, state, go):
    n = state.shape[0]
    per_step = RET_SAMPLE_SEQS
    assert n % per_step == 0
    col = pl.BlockSpec((D_GROUP, LANES), lambda b: (0, 0))
    row = pl.BlockSpec((per_step, 1, D_GROUP), lambda b: (b, 0, 0))
    st = pl.BlockSpec((per_step, N_HEADS, D_HEAD, D_HEAD), lambda b: (b, 0, 0, 0))
    return pl.pallas_call(
        _ret_sample_kernel,
        grid=(n // per_step,),
        in_specs=[col, col, row, row, st, pl.BlockSpec((1, D_HEAD), lambda b: (0, 0))],
        out_specs=[row, st],
        out_shape=[jax.ShapeDtypeStruct((n, 1, D_GROUP), F32),
                   jax.ShapeDtypeStruct((n, N_HEADS, D_HEAD, D_HEAD), F32)],
        compiler_params=pltpu.CompilerParams(dimension_semantics=("arbitrary",)),
        name="ret_sample",
    )(rqt, rkt, rv3, rg3, state, go)


def _rope_tables(pos):
    posf = np.asarray(pos, np.float64)[:, None]
    fr = RET_ROPE_THETA ** (-np.arange(0, D_HEAD, 2, dtype=np.float64) / D_HEAD)
    ang = posf * fr[None, :]
    cos, sin = np.cos(ang), np.sin(ang)
    rc = np.concatenate([cos, cos], axis=-1)
    rs = np.concatenate([-sin, sin], axis=-1)
    half = N_ROT_DIFF // 2
    fd = ROPE_THETA ** (-np.arange(0, N_ROT_DIFF, 2, dtype=np.float64) / N_ROT_DIFF)
    angd = posf * fd[None, :]
    cd, sd = np.cos(angd), np.sin(angd)
    n = posf.shape[0]
    ones = np.ones((n, D_MAP - N_ROT_DIFF))
    zeros = np.zeros((n, D_MAP - N_ROT_DIFF))
    z8 = np.zeros((n, half))
    dc = np.concatenate([cd, cd, ones], axis=-1)
    da = np.concatenate([-sd, z8, zeros], axis=-1)
    db = np.concatenate([z8, sd, zeros], axis=-1)
    twice = lambda t: np.concatenate([t, t], axis=-1)
    return tuple(jnp.asarray(t, F32) for t in (rc, rs, twice(dc), twice(da), twice(db)))


def kernel(x_prompt, x_sample, cache_k, cache_v, state_ret, page_table, norm_g, w_in, w_out,
           q_norm_g, k_norm_g, ret_out_g, diff_out_g, lambda_q1, lambda_k1, lambda_q2, lambda_k2):
    B, L, _ = x_prompt.shape
    n_s = x_sample.shape[0]
    n_pages = page_table.shape[1]
    past = n_pages * PAGE_SIZE
    l = 0

    ng = norm_g[l][None, :]
    w_in_bf = w_in[l].astype(BF16)
    w_out_bf = w_out[l].astype(BF16)
    gq = jnp.tile(q_norm_g[l], 2)[None, :]
    gk = jnp.tile(k_norm_g[l], 2)[None, :]
    go_ret = ret_out_g[l][None, :]
    go_diff = diff_out_g[l][None, :]
    lam_rows = jnp.stack([lambda_q1[l], lambda_k1[l], lambda_q2[l], lambda_k2[l]])
    lane = np.arange(LANES)
    seg = jnp.asarray(np.where((lane[:, None] // D_MAP) == (lane[None, :] // D_MAP), 1.0 / D_MAP, 0.0), BF16)

    tabs_p = _rope_tables(np.arange(L))
    rq, rk, rv, rg, dq, dk, dkb, dv, dvt, dg = _proj_prompt(x_prompt, ng, w_in_bf, tabs_p, gq, gk, seg)
    tabs_s = _rope_tables(np.full((n_s,), past))
    rqt, rkt, dqt, rv_s, rg_s, dq_s, dk_s, dv_s, dg_s = _proj_sample(
        x_sample.reshape(n_s, D_MODEL), ng, w_in_bf, tabs_s, gq, gk, seg)
    as_rows = lambda t: t.reshape(n_s, 1, D_GROUP)

    u_ret, s_fin = _ret_prompt(rq, rk, rv, rg, go_ret)
    u_ret_s, s_new = _ret_sample(rqt, rkt, as_rows(rv_s), as_rows(rg_s), state_ret[l], go_ret)

    n_phys = cache_k.shape[1]
    k_pages = jnp.transpose(cache_k[l], (0, 2, 3, 4, 1)).reshape(n_phys, D_GROUP, PAGE_SIZE)
    v_pages = cache_v[l].reshape(n_phys, PAGE_SIZE * N_HEADS, D_HEAD)
    u_diff, u_diff_s = _diff_attention(dq, dkb, dvt, dg, go_diff, lam_rows, page_table, dqt, as_rows(dq_s),
                                       as_rows(dk_s), as_rows(dv_s), as_rows(dg_s), k_pages, v_pages)

    y_prompt = _out_proj(x_prompt.reshape(B * L, D_MODEL), u_ret.reshape(B * L, D_GROUP),
                         u_diff.reshape(B * L, D_GROUP), w_out_bf, TOKEN_TILE).reshape(B, L, D_MODEL)
    y_sample = _out_proj(x_sample.reshape(n_s, D_MODEL), u_ret_s.reshape(n_s, D_GROUP),
                         u_diff_s.reshape(n_s, D_GROUP), w_out_bf, n_s).reshape(n_s, 1, D_MODEL)

    return (y_prompt, y_sample,
            dk.reshape(1, B, L, N_HEADS, 2, D_MAP), dv.reshape(1, B, L, N_HEADS, D_HEAD), s_fin[None],
            dk_s.reshape(1, n_s, 1, N_HEADS, 2, D_MAP), dv_s.reshape(1, n_s, 1, N_HEADS, D_HEAD), s_new[None])
```

```python
import functools
import math

import jax
import jax.numpy as jnp
import numpy as np
from jax import lax
from jax.experimental import pallas as pl
from jax.experimental.pallas import tpu as pltpu

F32 = jnp.float32
BF16 = jnp.bfloat16

D_MODEL = 1024
N_HEADS = 4
D_HEAD = 128
D_GROUP = N_HEADS * D_HEAD
D_MAP = 64
N_ROT_DIFF = 16
ROPE_THETA = 500000.0
RET_ROPE_THETA = 10000.0
PAGE_SIZE = 128
EPS = 1e-6
NEG_INF = -1e30
LAM_INIT = 0.8 - 0.6 * math.exp(-0.3 * 0)
RET_K_SCALE = D_HEAD ** -0.5
DIFF_Q_SCALE = D_MAP ** -0.5 * math.log2(math.e)
LOG_GAMMA = tuple(math.log(1.0 - 2.0 ** (-5.0 - h)) for h in range(N_HEADS))

LANES = 128
VMEM_LIMIT_BYTES = 56 * 1024 * 1024

TOKEN_TILE = 512
Q_CHUNK = 256
ONES_ROWS = 16
VT_ROWS = D_HEAD + ONES_ROWS
RET_CHUNK = 256
RET_TILE = 1024
OUT_TILE = 1024
RET_SAMPLE_SEQS = 8
PAGES_PER_GROUP = 4
DECODE_SLOTS = 4

_NT = (((1,), (1,)), ((), ()))
_TN = (((0,), (0,)), ((), ()))


def _head(h):
    return slice(h * D_HEAD, (h + 1) * D_HEAD)


def _silu(g):
    return g * (1.0 / (1.0 + jnp.exp(-g)))


def _rms_rows(x, gain):
    return x * lax.rsqrt(jnp.mean(x * x, axis=-1, keepdims=True) + EPS) * gain


def _rope_full(x, cos_t, sin_t):
    return x * cos_t + pltpu.roll(x, D_HEAD // 2, 1) * sin_t


def _rope_partial(x, c_t, a_t, b_t):
    half = N_ROT_DIFF // 2
    return x * c_t + pltpu.roll(x, LANES - half, 1) * a_t + pltpu.roll(x, half, 1) * b_t


def _map_rms(x, seg, gain):
    ms = jnp.dot((x * x).astype(BF16), seg, preferred_element_type=F32)
    return x * lax.rsqrt(ms + EPS) * gain


def _lambda(lam_ref):
    l = lam_ref[...]
    s1 = jnp.sum(l[0:1] * l[1:2], axis=-1, keepdims=True)
    s2 = jnp.sum(l[2:3] * l[3:4], axis=-1, keepdims=True)
    return jnp.exp(s1) - jnp.exp(s2) + LAM_INIT


def _normed_input(x_ref, ng_ref):
    x = x_ref[...]
    return _rms_rows(x, ng_ref[...]).astype(BF16)


def _z_group(hb, w_ref, g):
    return jnp.dot(hb, w_ref[:, g * D_GROUP:(g + 1) * D_GROUP], preferred_element_type=F32)


def _proj_prompt_kernel(x_ref, ng_ref, w_ref, rc_ref, rs_ref, dc_ref, da_ref, db_ref, gq_ref, gk_ref, seg_ref,
                        rq_ref, rk_ref, rv_ref, rg_ref, dq_ref, dk_ref, dkb_ref, dv_ref, dvt_ref, dg_ref):
    hb = _normed_input(x_ref, ng_ref)
    rc, rs = rc_ref[...], rs_ref[...]
    dc, da, db = dc_ref[...], da_ref[...], db_ref[...]
    seg = seg_ref[...]

    z = _z_group(hb, w_ref, 0)
    for h in range(N_HEADS):
        rq_ref[:, _head(h)] = _rope_full(z[:, _head(h)], rc, rs).astype(BF16)
    z = _z_group(hb, w_ref, 1)
    rck, rsk = rc * RET_K_SCALE, rs * RET_K_SCALE
    for h in range(N_HEADS):
        rk_ref[:, _head(h)] = _rope_full(z[:, _head(h)], rck, rsk).astype(BF16)
    rv_ref[...] = _z_group(hb, w_ref, 2).astype(BF16)
    rg_ref[...] = _z_group(hb, w_ref, 3).astype(BF16)

    z = _z_group(hb, w_ref, 4)
    dcq, daq, dbq = dc * DIFF_Q_SCALE, da * DIFF_Q_SCALE, db * DIFF_Q_SCALE
    for h in range(N_HEADS):
        qn = _map_rms(z[:, _head(h)], seg, gq_ref[...])
        dq_ref[:, _head(h)] = _rope_partial(qn, dcq, daq, dbq).astype(BF16)
    z = _z_group(hb, w_ref, 5)
    for h in range(N_HEADS):
        kn = _map_rms(z[:, _head(h)], seg, gk_ref[...])
        kr = _rope_partial(kn, dc, da, db)
        dk_ref[:, _head(h)] = kr
        dkb_ref[:, _head(h)] = kr.astype(BF16)
    z = _z_group(hb, w_ref, 6)
    tm = z.shape[0]
    ones = jnp.ones((ONES_ROWS, tm), BF16)
    for h in range(N_HEADS):
        zh = z[:, _head(h)]
        dv_ref[pl.ds(h, tm, stride=N_HEADS), :] = zh
        dvt_ref[h, :D_HEAD, :] = zh.T.astype(BF16)
        dvt_ref[h, D_HEAD:, :] = ones
    dg_ref[...] = _z_group(hb, w_ref, 7).astype(BF16)


def _proj_prompt(x, ng, w_bf, tabs, gq, gk, seg):
    B, L, _ = x.shape
    tm = TOKEN_TILE
    nt = L // tm
    tok = lambda width: pl.BlockSpec((None, tm, width), lambda b, i: (b, i, 0))
    tab = pl.BlockSpec((tm, LANES), lambda b, i: (i, 0))
    const = lambda shape: pl.BlockSpec(shape, lambda b, i: (0,) * len(shape))
    bf_out = jax.ShapeDtypeStruct((B, L, D_GROUP), BF16)
    f32_out = jax.ShapeDtypeStruct((B, L, D_GROUP), F32)
    return pl.pallas_call(
        _proj_prompt_kernel,
        grid=(B, nt),
        in_specs=[tok(D_MODEL), const((1, D_MODEL)), const((D_MODEL, 8 * D_GROUP)),
                  tab, tab, tab, tab, tab,
                  const((1, LANES)), const((1, LANES)), const((LANES, LANES))],
        out_specs=[tok(D_GROUP), tok(D_GROUP), tok(D_GROUP), tok(D_GROUP),
                   tok(D_GROUP), tok(D_GROUP), tok(D_GROUP),
                   pl.BlockSpec((None, tm * N_HEADS, D_HEAD), lambda b, i: (b, i, 0)),
                   pl.BlockSpec((None, None, N_HEADS, VT_ROWS, tm), lambda b, i: (b, i, 0, 0, 0)),
                   tok(D_GROUP)],
        out_shape=[bf_out, bf_out, bf_out, bf_out, bf_out, f32_out, bf_out,
                   jax.ShapeDtypeStruct((B, L * N_HEADS, D_HEAD), F32),
                   jax.ShapeDtypeStruct((B, nt, N_HEADS, VT_ROWS, tm), BF16), bf_out],
        compiler_params=pltpu.CompilerParams(
            dimension_semantics=("arbitrary", "arbitrary"), vmem_limit_bytes=VMEM_LIMIT_BYTES),
        name="proj_prompt",
    )(x, ng, w_bf, *tabs, gq, gk, seg)


def _ret_prompt_kernel(q_ref, k_ref, v_ref, g_ref, go_ref, u_ref, sfin_ref, s_scr, dm_scr, qd_scr, kd_scr):
    C = dm_scr.shape[-1]
    n_sub = q_ref.shape[0] // C
    b, c = pl.program_id(0), pl.program_id(1)

    @pl.when((b == 0) & (c == 0))
    def _():
        rel = (lax.broadcasted_iota(jnp.int32, (C, C), 0) - lax.broadcasted_iota(jnp.int32, (C, C), 1)).astype(F32)
        idx = lax.broadcasted_iota(jnp.int32, (C, D_HEAD), 0).astype(F32)
        for h in range(N_HEADS):
            lg = LOG_GAMMA[h]
            dm_scr[h] = jnp.where(rel >= 0, jnp.exp(lg * jnp.maximum(rel, 0.0)), 0.0)
            qd_scr[h] = jnp.exp(lg * (idx + 1.0))
            kd_scr[h] = jnp.exp(lg * (C - 1.0 - idx))

    @pl.when(c == 0)
    def _():
        s_scr[...] = jnp.zeros_like(s_scr)

    state = [s_scr[h] for h in range(N_HEADS)]
    for sub in range(n_sub):
        rows = slice(sub * C, (sub + 1) * C)
        for h in range(N_HEADS):
            lg = LOG_GAMMA[h]
            q, k, v = q_ref[rows, _head(h)], k_ref[rows, _head(h)], v_ref[rows, _head(h)]
            s_old = state[h]
            inner = lax.dot_general(q, k, _NT, preferred_element_type=F32) * dm_scr[h]
            q_dec = (q.astype(F32) * qd_scr[h]).astype(BF16)
            o = (jnp.dot(inner.astype(BF16), v, preferred_element_type=F32)
                 + jnp.dot(q_dec, s_old.astype(BF16), preferred_element_type=F32))
            k_dec = (k.astype(F32) * kd_scr[h]).astype(BF16)
            state[h] = math.exp(lg * C) * s_old + lax.dot_general(k_dec, v, _TN, preferred_element_type=F32)
            u_ref[rows, _head(h)] = (_rms_rows(o, go_ref[...])
                                     * _silu(g_ref[rows, _head(h)].astype(F32))).astype(BF16)
    for h in range(N_HEADS):
        s_scr[h] = state[h]

    @pl.when(c == pl.num_programs(1) - 1)
    def _():
        sfin_ref[...] = s_scr[...]


def _ret_prompt(rq, rk, rv, rg, go):
    B, L, _ = rq.shape
    C, T = RET_CHUNK, RET_TILE
    assert T % C == 0 and L % T == 0
    tok = pl.BlockSpec((None, T, D_GROUP), lambda b, c: (b, c, 0))
    return pl.pallas_call(
        _ret_prompt_kernel,
        grid=(B, L // T),
        in_specs=[tok, tok, tok, tok, pl.BlockSpec((1, D_HEAD), lambda b, c: (0, 0))],
        out_specs=[tok, pl.BlockSpec((None, N_HEADS, D_HEAD, D_HEAD), lambda b, c: (b, 0, 0, 0))],
        out_shape=[jax.ShapeDtypeStruct((B, L, D_GROUP), BF16),
                   jax.ShapeDtypeStruct((B, N_HEADS, D_HEAD, D_HEAD), F32)],
        scratch_shapes=[pltpu.VMEM((N_HEADS, D_HEAD, D_HEAD), F32), pltpu.VMEM((N_HEADS, C, C), F32),
                        pltpu.VMEM((N_HEADS, C, D_HEAD), F32), pltpu.VMEM((N_HEADS, C, D_HEAD), F32)],
        compiler_params=pltpu.CompilerParams(
            dimension_semantics=("arbitrary", "arbitrary"), vmem_limit_bytes=VMEM_LIMIT_BYTES),
        name="ret_prompt",
    )(rq, rk, rv, rg, go)


class _Decode:
    def __init__(self, pt_ref, qt_ref, q_ref, kn_ref, vn_ref, g_ref, go_ref, lam_ref, k_hbm, v_hbm, us_ref,
                 kbuf, vbuf, sem, qb_scr, acc_scr, m_scr, l_scr):
        self.__dict__.update(locals())
        n_seq, n_pages = pt_ref.shape
        self.groups_per_seq = n_pages // PAGES_PER_GROUP
        self.n_groups = n_seq * self.groups_per_seq
        self.n_maps = 2 * N_HEADS

    def _copies(self, gg):
        seq, first = lax.div(gg, self.groups_per_seq), lax.rem(gg, self.groups_per_seq) * PAGES_PER_GROUP
        slot = lax.rem(gg, DECODE_SLOTS)
        out = []
        for i in range(PAGES_PER_GROUP):
            page = self.pt_ref[seq, first + i]
            out.append(pltpu.make_async_copy(self.k_hbm.at[page], self.kbuf.at[slot, i], self.sem.at[slot]))
            out.append(pltpu.make_async_copy(self.v_hbm.at[page], self.vbuf.at[slot, i], self.sem.at[slot]))
        return out

    def start(self, gg):
        for cp in self._copies(gg):
            cp.start()

    def wait(self, gg):
        for cp in self._copies(gg):
            cp.wait()

    def begin_sequence(self, seq):
        lane = lax.broadcasted_iota(jnp.int32, (D_GROUP, LANES), 1)
        q_col = jnp.sum(jnp.where(lane == seq, self.qt_ref[...], 0.0), axis=1, keepdims=True)
        self.qb_scr[...] = jnp.broadcast_to(q_col, (D_GROUP, LANES))
        self.m_scr[...] = jnp.full(self.m_scr.shape, NEG_INF, F32)
        self.l_scr[...] = jnp.zeros_like(self.l_scr)
        self.acc_scr[...] = jnp.zeros_like(self.acc_scr)

    def group(self, slot):
        row_id = lax.broadcasted_iota(jnp.int32, (self.n_maps, PAGE_SIZE), 0)
        s_pages = [jnp.zeros((self.n_maps, PAGE_SIZE), F32)] * PAGES_PER_GROUP
        for r in range(self.n_maps):
            rows = slice(r * D_MAP, (r + 1) * D_MAP)
            q_r = self.qb_scr[rows, :]
            for i in range(PAGES_PER_GROUP):
                s_r = jnp.sum(self.kbuf[slot, i, rows, :] * q_r, axis=0, keepdims=True)
                s_pages[i] = jnp.where(row_id == r, s_r, s_pages[i])
        s = jnp.concatenate(s_pages, axis=1)
        m_old = self.m_scr[...]
        m_new = jnp.maximum(m_old, jnp.max(s, axis=1, keepdims=True))
        alpha = jnp.exp2(m_old - m_new)
        p = jnp.exp2(s - m_new)
        self.l_scr[...] = alpha * self.l_scr[...] + jnp.sum(p, axis=1, keepdims=True)
        p_bf = p.astype(BF16)
        pv = []
        for h in range(N_HEADS):
            v_h = jnp.concatenate([self.vbuf[slot, i, pl.ds(h, PAGE_SIZE, stride=N_HEADS), :]
                                   for i in range(PAGES_PER_GROUP)], axis=0)
            pv.append(jnp.dot(p_bf, v_h.astype(BF16), preferred_element_type=F32))
        self.acc_scr[...] = alpha * self.acc_scr[...] + jnp.concatenate(pv, axis=1)
        self.m_scr[...] = m_new

    def end_sequence(self, seq):
        n_maps = self.n_maps
        r_idx = lax.broadcasted_iota(jnp.int32, (n_maps, D_GROUP), 0)
        c_idx = lax.broadcasted_iota(jnp.int32, (n_maps, D_GROUP), 1)
        in_map = (c_idx >= r_idx * D_MAP) & (c_idx < (r_idx + 1) * D_MAP)
        q_bd = jnp.where(in_map, jnp.broadcast_to(self.q_ref[seq], (n_maps, D_GROUP)), 0.0)
        s_new = jnp.sum(q_bd * self.kn_ref[seq], axis=1, keepdims=True)
        m_old = self.m_scr[...]
        m_fin = jnp.maximum(m_old, s_new)
        alpha = jnp.exp2(m_old - m_fin)
        p_new = jnp.exp2(s_new - m_fin)
        l_fin = alpha * self.l_scr[...] + p_new
        acc = (alpha * self.acc_scr[...] + p_new * self.vn_ref[seq]) * (1.0 / l_fin)
        lam = _lambda(self.lam_ref)
        gate = self.g_ref[seq]
        for h in range(N_HEADS):
            o = acc[2 * h:2 * h + 1, _head(h)] - lam * acc[2 * h + 1:2 * h + 2, _head(h)]
            on = _rms_rows(o, self.go_ref[...]) * (1.0 - LAM_INIT)
            self.us_ref[seq, :, _head(h)] = on * _silu(gate[:, _head(h)])


def _diff_prompt_kernel(tq_tab, tk_tab, pt_ref, q_ref, k_ref, vt_ref, g_ref, go_ref, lam_ref,
                        qts_ref, qs_ref, kns_ref, vns_ref, gs_ref, k_hbm, v_hbm, u_ref, us_ref,
                        acc_scr, m_scr, sa_scr, mta_scr, sb_scr, mtb_scr,
                        kbuf, vbuf, dsem, qb_scr, dacc_scr, dm_scr, dl_scr):
    tq = tk = vt_ref.shape[-1]
    n_half = tq // Q_CHUNK
    n_steps = tq_tab.shape[0] - 1
    chunks = [(mp, hf) for mp in range(2) for hf in range(n_half)]
    lane = lax.broadcasted_iota(jnp.int32, (Q_CHUNK, D_HEAD), 1)
    in_map = (lane < D_MAP, lane >= D_MAP)
    dec = _Decode(pt_ref, qts_ref, qs_ref, kns_ref, vns_ref, gs_ref, go_ref, lam_ref, k_hbm, v_hbm, us_ref,
                  kbuf, vbuf, dsem, qb_scr, dacc_scr, dm_scr, dl_scr)
    first_g = (pl.program_id(0) * pl.num_programs(1) + pl.program_id(1)) * n_steps

    m_scr[...] = jnp.full(m_scr.shape, NEG_INF, F32)
    acc_scr[...] = jnp.zeros_like(acc_scr)

    def scores(f, s_buf, mt_buf):
        q0 = pl.multiple_of(tq_tab[f] * tq, tq)
        k = k_ref[pl.ds(pl.multiple_of(tk_tab[f] * tk, tk), tk), :]
        for c, (mp, hf) in enumerate(chunks):
            qc = q_ref[pl.ds(q0 + hf * Q_CHUNK, Q_CHUNK), :]
            qc = jnp.where(in_map[mp], qc, jnp.zeros_like(qc))
            s = lax.dot_general(k, qc, _NT, preferred_element_type=F32)
            s_buf[c] = s
            mt_buf[:, c * Q_CHUNK:(c + 1) * Q_CHUNK] = jnp.max(s, axis=0, keepdims=True)

    def consume(f, s_buf, mt_buf, diagonal):
        j = tk_tab[f]
        m_all, acc_all = m_scr[...], acc_scr[...]
        m_out, acc_out = [], []
        for c, (mp, hf) in enumerate(chunks):
            cols = slice(c * Q_CHUNK, (c + 1) * Q_CHUNK)
            m_old = m_all[:, cols]
            if diagonal:
                rows = (hf + 1) * Q_CHUNK
                s = s_buf[c, :rows, :]
                k_pos = lax.broadcasted_iota(jnp.int32, s.shape, 0)
                q_pos = hf * Q_CHUNK + lax.broadcasted_iota(jnp.int32, s.shape, 1)
                s = jnp.where(k_pos <= q_pos, s, NEG_INF)
                m_new = jnp.maximum(m_old, jnp.max(s, axis=0, keepdims=True))
            else:
                rows = tk
                s = s_buf[c]
                m_new = jnp.maximum(m_old, mt_buf[:, cols])
            alpha = jnp.exp2(m_old - m_new)
            p = jnp.exp2((s - m_new).astype(BF16))
            pv = jnp.dot(vt_ref[j, :, :rows], p, preferred_element_type=F32)
            acc_out.append(alpha * acc_all[:, cols] + pv)
            m_out.append(m_new)
        if not diagonal:
            m_scr[...] = jnp.concatenate(m_out, axis=1)
            acc_scr[...] = jnp.concatenate(acc_out, axis=1)
            return
        acc = jnp.concatenate(acc_out, axis=1)
        o_all = acc[:D_HEAD] * (1.0 / acc[D_HEAD:D_HEAD + 1])
        o_t = o_all[:, :tq] - _lambda(lam_ref) * o_all[:, tq:]
        on = _rms_rows(o_t.T, go_ref[...]) * (1.0 - LAM_INIT)
        q_rows = pl.ds(pl.multiple_of(j * tq, tq), tq)
        u_ref[q_rows, :] = (on * _silu(g_ref[q_rows, :].astype(F32))).astype(BF16)
        m_scr[...] = jnp.full(m_scr.shape, NEG_INF, F32)
        acc_scr[...] = jnp.zeros_like(acc_scr)

    def step(f, cur, nxt):
        g = first_g + f
        seq, grp = lax.div(g, dec.groups_per_seq), lax.rem(g, dec.groups_per_seq)
        live = g < dec.n_groups

        @pl.when(g == 0)
        def _():
            dec.start(0)
            dec.start(1)

        @pl.when(g + 2 < dec.n_groups)
        def _():
            dec.start(g + 2)

        @pl.when(live)
        def _():
            dec.wait(g)

        @pl.when(live & (grp == 0))
        def _():
            dec.begin_sequence(seq)

        diagonal = tq_tab[f] == tk_tab[f]
        slot = lax.rem(g, DECODE_SLOTS)

        @pl.when(diagonal)
        def _():
            scores(f + 1, *nxt)
            consume(f, *cur, diagonal=True)
            dec.group(slot)

        @pl.when(jnp.logical_not(diagonal))
        def _():
            scores(f + 1, *nxt)
            consume(f, *cur, diagonal=False)
            dec.group(slot)

        @pl.when(live & (grp == dec.groups_per_seq - 1))
        def _():
            dec.end_sequence(seq)

    buf_a, buf_b = (sa_scr, mta_scr), (sb_scr, mtb_scr)
    scores(0, *buf_a)

    def pair(i, carry):
        step(2 * i, buf_a, buf_b)
        step(2 * i + 1, buf_b, buf_a)
        return carry

    assert n_steps % 2 == 0
    lax.fori_loop(0, n_steps // 2, pair, 0)


def _diff_attention(dq, dkb, dvt, dg, go, lam_rows, page_table, dqt_s, dq_s, dk_s, dv_s, dg_s, k_pages, v_pages):
    B, L, _ = dq.shape
    nk, tk = dvt.shape[1], dvt.shape[4]
    n_s, n_pages = page_table.shape
    assert L == nk * tk and tk % Q_CHUNK == 0 and n_pages % PAGES_PER_GROUP == 0
    pairs = [(qi, j) for qi in range(nk) for j in range(qi + 1)] + [(0, 0)]
    n_steps = len(pairs) - 1
    assert B * N_HEADS * n_steps >= n_s * (n_pages // PAGES_PER_GROUP) >= DECODE_SLOTS
    tq_tab = jnp.asarray([p[0] for p in pairs], jnp.int32)
    tk_tab = jnp.asarray([p[1] for p in pairs], jnp.int32)
    seq = pl.BlockSpec((None, L, D_HEAD), lambda b, h, *_: (b, 0, h))
    whole = lambda shape: pl.BlockSpec(shape, lambda b, h, *_: (0,) * len(shape))
    rows_s = whole((n_s, 1, D_GROUP))
    hbm = pl.BlockSpec(memory_space=pl.ANY)
    n_maps = 2 * N_HEADS
    group_buf = pltpu.VMEM((DECODE_SLOTS, PAGES_PER_GROUP, D_GROUP, LANES), F32)
    grid_spec = pltpu.PrefetchScalarGridSpec(
        num_scalar_prefetch=3,
        grid=(B, N_HEADS),
        in_specs=[seq, seq,
                  pl.BlockSpec((None, nk, None, VT_ROWS, tk), lambda b, h, *_: (b, 0, h, 0, 0)),
                  seq, whole((1, D_HEAD)), whole((4, D_MAP)),
                  whole((D_GROUP, LANES)), rows_s, rows_s, rows_s, rows_s, hbm, hbm],
        out_specs=[seq, rows_s],
        scratch_shapes=[pltpu.VMEM((VT_ROWS, 2 * tk), F32), pltpu.VMEM((1, 2 * tk), F32)]
                       + [pltpu.VMEM((2 * tk // Q_CHUNK, tk, Q_CHUNK), F32), pltpu.VMEM((1, 2 * tk), F32)] * 2
                       + [group_buf, group_buf, pltpu.SemaphoreType.DMA((DECODE_SLOTS,)),
                          pltpu.VMEM((D_GROUP, LANES), F32), pltpu.VMEM((n_maps, D_GROUP), F32),
                          pltpu.VMEM((n_maps, 1), F32), pltpu.VMEM((n_maps, 1), F32)],
    )
    return pl.pallas_call(
        _diff_prompt_kernel,
        grid_spec=grid_spec,
        out_shape=[jax.ShapeDtypeStruct((B, L, D_GROUP), BF16), jax.ShapeDtypeStruct((n_s, 1, D_GROUP), F32)],
        compiler_params=pltpu.CompilerParams(
            dimension_semantics=("arbitrary", "arbitrary"), vmem_limit_bytes=VMEM_LIMIT_BYTES),
        name="diff_attention",
    )(tq_tab, tk_tab, page_table, dq, dkb, dvt, dg, go, lam_rows, dqt_s, dq_s, dk_s, dv_s, dg_s, k_pages, v_pages)


def _out_proj_kernel(x_ref, ur_ref, ud_ref, w_ref, y_ref):
    y_ref[...] = (x_ref[...]
                  + jnp.dot(ur_ref[...].astype(BF16), w_ref[:D_GROUP, :], preferred_element_type=F32)
                  + jnp.dot(ud_ref[...].astype(BF16), w_ref[D_GROUP:, :], preferred_element_type=F32))


def _out_proj(x2d, ur, ud, w_bf, tm):
    T = x2d.shape[0]
    tok = lambda width: pl.BlockSpec((tm, width), lambda i: (i, 0))
    return pl.pallas_call(
        _out_proj_kernel,
        grid=(T // tm,),
        in_specs=[tok(D_MODEL), tok(D_GROUP), tok(D_GROUP),
                  pl.BlockSpec((2 * D_GROUP, D_MODEL), lambda i: (0, 0))],
        out_specs=tok(D_MODEL),
        out_shape=jax.ShapeDtypeStruct((T, D_MODEL), F32),
        compiler_params=pltpu.CompilerParams(
            dimension_semantics=("arbitrary",), vmem_limit_bytes=VMEM_LIMIT_BYTES),
        name="out_proj",
    )(x2d, ur, ud, w_bf)


def _proj_sample_kernel(x_ref, ng_ref, w_ref, rc_ref, rs_ref, dc_ref, da_ref, db_ref, gq_ref, gk_ref, seg_ref,
                        rqt_ref, rkt_ref, dqt_ref, rv_ref, rg_ref, dq_ref, dk_ref, dv_ref, dg_ref):
    hb = _normed_input(x_ref, ng_ref)
    n = hb.shape[0]
    rc, rs = rc_ref[...], rs_ref[...]
    dc, da, db = dc_ref[...], da_ref[...], db_ref[...]
    seg = seg_ref[...]
    pad = jnp.zeros((LANES - n, D_HEAD), F32)

    def transposed(xh):
        return jnp.concatenate([xh, pad], axis=0).T

    z = _z_group(hb, w_ref, 0)
    for h in range(N_HEADS):
        rqt_ref[_head(h), :] = transposed(_rope_full(z[:, _head(h)], rc, rs))
    z = _z_group(hb, w_ref, 1)
    rck, rsk = rc * RET_K_SCALE, rs * RET_K_SCALE
    for h in range(N_HEADS):
        rkt_ref[_head(h), :] = transposed(_rope_full(z[:, _head(h)], rck, rsk))
    rv_ref[...] = _z_group(hb, w_ref, 2)
    rg_ref[...] = _z_group(hb, w_ref, 3)
    z = _z_group(hb, w_ref, 4)
    dcq, daq, dbq = dc * DIFF_Q_SCALE, da * DIFF_Q_SCALE, db * DIFF_Q_SCALE
    for h in range(N_HEADS):
        qh = _rope_partial(_map_rms(z[:, _head(h)], seg, gq_ref[...]), dcq, daq, dbq)
        dq_ref[:, _head(h)] = qh
        dqt_ref[_head(h), :] = transposed(qh)
    z = _z_group(hb, w_ref, 5)
    for h in range(N_HEADS):
        dk_ref[:, _head(h)] = _rope_partial(_map_rms(z[:, _head(h)], seg, gk_ref[...]), dc, da, db)
    dv_ref[...] = _z_group(hb, w_ref, 6)
    dg_ref[...] = _z_group(hb, w_ref, 7)


def _proj_sample(x, ng, w_bf, tabs, gq, gk, seg):
    n = x.shape[0]
    full = lambda shape: pl.BlockSpec(shape, lambda i: (0,) * len(shape))
    row_out = jax.ShapeDtypeStruct((n, D_GROUP), F32)
    col_out = jax.ShapeDtypeStruct((D_GROUP, LANES), F32)
    return pl.pallas_call(
        _proj_sample_kernel,
        grid=(1,),
        in_specs=[full((n, D_MODEL)), full((1, D_MODEL)), full((D_MODEL, 8 * D_GROUP))]
                 + [full((n, LANES))] * 5 + [full((1, LANES)), full((1, LANES)), full((LANES, LANES))],
        out_specs=[full((D_GROUP, LANES))] * 3 + [full((n, D_GROUP))] * 6,
        out_shape=[col_out] * 3 + [row_out] * 6,
        compiler_params=pltpu.CompilerParams(
            dimension_semantics=("arbitrary",), vmem_limit_bytes=VMEM_LIMIT_BYTES),
        name="proj_sample",
    )(x, ng, w_bf, *tabs, gq, gk, seg)


def _ret_sample_kernel(qt_ref, kt_ref, v_ref, g_ref, s_ref, go_ref, u_ref, snew_ref):
    per_step = v_ref.shape[0]
    lane = lax.broadcasted_iota(jnp.int32, (D_HEAD, LANES), 1)

    def column(ref, h, b):
        return jnp.sum(jnp.where(lane == b, ref[_head(h), :], 0.0), axis=1, keepdims=True)

    for i in range(per_step):
        b = pl.program_id(0) * per_step + i
        for h in range(N_HEADS):
            gamma = math.exp(LOG_GAMMA[h])
            q_col, k_col = column(qt_ref, h, b), column(kt_ref, h, b)
            v_row = v_ref[i, :, _head(h)]
            s_old = s_ref[i, h]
            inner = jnp.sum(q_col * k_col, axis=0, keepdims=True)
            o = inner * v_row + jnp.sum((q_col * gamma) * s_old, axis=0, keepdims=True)
            snew_ref[i, h] = gamma * s_old + k_col * v_row
            u_ref[i, :, _head(h)] = _rms_rows(o, go_ref[...]) * _silu(g_ref[i, :, _head(h)])


def _ret_sample(rqt, rkt, rv3, rg3, state, go):
    n = state.shape[0]
    per_step = RET_SAMPLE_SEQS
    assert n % per_step == 0
    col = pl.BlockSpec((D_GROUP, LANES), lambda b: (0, 0))
    row = pl.BlockSpec((per_step, 1, D_GROUP), lambda b: (b, 0, 0))
    st = pl.BlockSpec((per_step, N_HEADS, D_HEAD, D_HEAD), lambda b: (b, 0, 0, 0))
    return pl.pallas_call(
        _ret_sample_kernel,
        grid=(n // per_step,),
        in_specs=[col, col, row, row, st, pl.BlockSpec((1, D_HEAD), lambda b: (0, 0))],
        out_specs=[row, st],
        out_shape=[jax.ShapeDtypeStruct((n, 1, D_GROUP), F32),
                   jax.ShapeDtypeStruct((n, N_HEADS, D_HEAD, D_HEAD), F32)],
        compiler_params=pltpu.CompilerParams(dimension_semantics=("arbitrary",)),
        name="ret_sample",
    )(rqt, rkt, rv3, rg3, state, go)


def _rope_tables(pos):
    posf = np.asarray(pos, np.float64)[:, None]
    fr = RET_ROPE_THETA ** (-np.arange(0, D_HEAD, 2, dtype=np.float64) / D_HEAD)
    ang = posf * fr[None, :]
    cos, sin = np.cos(ang), np.sin(ang)
    rc = np.concatenate([cos, cos], axis=-1)
    rs = np.concatenate([-sin, sin], axis=-1)
    half = N_ROT_DIFF // 2
    fd = ROPE_THETA ** (-np.arange(0, N_ROT_DIFF, 2, dtype=np.float64) / N_ROT_DIFF)
    angd = posf * fd[None, :]
    cd, sd = np.cos(angd), np.sin(angd)
    n = posf.shape[0]
    ones = np.ones((n, D_MAP - N_ROT_DIFF))
    zeros = np.zeros((n, D_MAP - N_ROT_DIFF))
    z8 = np.zeros((n, half))
    dc = np.concatenate([cd, cd, ones], axis=-1)
    da = np.concatenate([-sd, z8, zeros], axis=-1)
    db = np.concatenate([z8, sd, zeros], axis=-1)
    twice = lambda t: np.concatenate([t, t], axis=-1)
    return tuple(jnp.asarray(t, F32) for t in (rc, rs, twice(dc), twice(da), twice(db)))


def kernel(x_prompt, x_sample, cache_k, cache_v, state_ret, page_table, norm_g, w_in, w_out,
           q_norm_g, k_norm_g, ret_out_g, diff_out_g, lambda_q1, lambda_k1, lambda_q2, lambda_k2):
    B, L, _ = x_prompt.shape
    n_s = x_sample.shape[0]
    n_pages = page_table.shape[1]
    past = n_pages * PAGE_SIZE
    l = 0

    ng = norm_g[l][None, :]
    w_in_bf = w_in[l].astype(BF16)
    w_out_bf = w_out[l].astype(BF16)
    gq = jnp.tile(q_norm_g[l], 2)[None, :]
    gk = jnp.tile(k_norm_g[l], 2)[None, :]
    go_ret = ret_out_g[l][None, :]
    go_diff = diff_out_g[l][None, :]
    lam_rows = jnp.stack([lambda_q1[l], lambda_k1[l], lambda_q2[l], lambda_k2[l]])
    lane = np.arange(LANES)
    seg = jnp.asarray(np.where((lane[:, None] // D_MAP) == (lane[None, :] // D_MAP), 1.0 / D_MAP, 0.0), BF16)

    tabs_p = _rope_tables(np.arange(L))
    rq, rk, rv, rg, dq, dk, dkb, dv, dvt, dg = _proj_prompt(x_prompt, ng, w_in_bf, tabs_p, gq, gk, seg)
    tabs_s = _rope_tables(np.full((n_s,), past))
    rqt, rkt, dqt, rv_s, rg_s, dq_s, dk_s, dv_s, dg_s = _proj_sample(
        x_sample.reshape(n_s, D_MODEL), ng, w_in_bf, tabs_s, gq, gk, seg)
    as_rows = lambda t: t.reshape(n_s, 1, D_GROUP)

    u_ret, s_fin = _ret_prompt(rq, rk, rv, rg, go_ret)
    u_ret_s, s_new = _ret_sample(rqt, rkt, as_rows(rv_s), as_rows(rg_s), state_ret[l], go_ret)

    n_phys = cache_k.shape[1]
    k_pages = jnp.transpose(cache_k[l], (0, 2, 3, 4, 1)).reshape(n_phys, D_GROUP, PAGE_SIZE)
    v_pages = cache_v[l].reshape(n_phys, PAGE_SIZE * N_HEADS, D_HEAD)
    u_diff, u_diff_s = _diff_attention(dq, dkb, dvt, dg, go_diff, lam_rows, page_table, dqt, as_rows(dq_s),
                                       as_rows(dk_s), as_rows(dv_s), as_rows(dg_s), k_pages, v_pages)

    y_prompt = _out_proj(x_prompt.reshape(B * L, D_MODEL), u_ret.reshape(B * L, D_GROUP),
                         u_diff.reshape(B * L, D_GROUP), w_out_bf, OUT_TILE).reshape(B, L, D_MODEL)
    y_sample = _out_proj(x_sample.reshape(n_s, D_MODEL), u_ret_s.reshape(n_s, D_GROUP),
                         u_diff_s.reshape(n_s, D_GROUP), w_out_bf, n_s).reshape(n_s, 1, D_MODEL)

    return (y_prompt, y_sample,
            dk.reshape(1, B, L, N_HEADS, 2, D_MAP), dv.reshape(1, B, L, N_HEADS, D_HEAD), s_fin[None],
            dk_s.reshape(1, n_s, 1, N_HEADS, 2, D_MAP), dv_s.reshape(1, n_s, 1, N_HEADS, D_HEAD), s_new[None])
```

```python
import functools
import math

import jax
import jax.numpy as jnp
import numpy as np
from jax import lax
from jax.experimental import pallas as pl
from jax.experimental.pallas import tpu as pltpu

F32 = jnp.float32
BF16 = jnp.bfloat16

D_MODEL = 1024
N_HEADS = 4
D_HEAD = 128
D_GROUP = N_HEADS * D_HEAD
D_MAP = 64
N_ROT_DIFF = 16
ROPE_THETA = 500000.0
RET_ROPE_THETA = 10000.0
PAGE_SIZE = 128
EPS = 1e-6
NEG_INF = -1e30
LAM_INIT = 0.8 - 0.6 * math.exp(-0.3 * 0)
RET_K_SCALE = D_HEAD ** -0.5
DIFF_Q_SCALE = D_MAP ** -0.5 * math.log2(math.e)
LOG_GAMMA = tuple(math.log(1.0 - 2.0 ** (-5.0 - h)) for h in range(N_HEADS))

LANES = 128
VMEM_LIMIT_BYTES = 56 * 1024 * 1024

TOKEN_TILE = 512
Q_CHUNK = 256
ONES_ROWS = 16
VT_ROWS = D_HEAD + ONES_ROWS
RET_CHUNK = 256
RET_TILE = 1024
OUT_TILE = 1024
RET_SAMPLE_SEQS = 8
PAGES_PER_GROUP = 4
DECODE_SLOTS = 4

_NT = (((1,), (1,)), ((), ()))
_TN = (((0,), (0,)), ((), ()))


def _head(h):
    return slice(h * D_HEAD, (h + 1) * D_HEAD)


def _silu(g):
    return g * (1.0 / (1.0 + jnp.exp(-g)))


def _rms_rows(x, gain):
    return x * lax.rsqrt(jnp.mean(x * x, axis=-1, keepdims=True) + EPS) * gain


def _rope_full(x, cos_t, sin_t):
    return x * cos_t + pltpu.roll(x, D_HEAD // 2, 1) * sin_t


def _rope_partial(x, c_t, a_t, b_t):
    half = N_ROT_DIFF // 2
    return x * c_t + pltpu.roll(x, LANES - half, 1) * a_t + pltpu.roll(x, half, 1) * b_t


def _map_rms(x, seg, gain):
    ms = jnp.dot((x * x).astype(BF16), seg, preferred_element_type=F32)
    return x * lax.rsqrt(ms + EPS) * gain


def _lambda(lam_ref):
    l = lam_ref[...]
    s1 = jnp.sum(l[0:1] * l[1:2], axis=-1, keepdims=True)
    s2 = jnp.sum(l[2:3] * l[3:4], axis=-1, keepdims=True)
    return jnp.exp(s1) - jnp.exp(s2) + LAM_INIT


def _normed_input(x_ref, ng_ref):
    x = x_ref[...]
    return _rms_rows(x, ng_ref[...]).astype(BF16)


def _z_group(hb, w_ref, g):
    return jnp.dot(hb, w_ref[:, g * D_GROUP:(g + 1) * D_GROUP], preferred_element_type=F32)


def _proj_prompt_kernel(x_ref, ng_ref, w_ref, rc_ref, rs_ref, dc_ref, da_ref, db_ref, gq_ref, gk_ref, seg_ref,
                        rq_ref, rk_ref, rv_ref, rg_ref, dq_ref, dk_ref, dkb_ref, dv_ref, dvt_ref, dg_ref):
    hb = _normed_input(x_ref, ng_ref)
    rc, rs = rc_ref[...], rs_ref[...]
    dc, da, db = dc_ref[...], da_ref[...], db_ref[...]
    seg = seg_ref[...]

    z = _z_group(hb, w_ref, 0)
    for h in range(N_HEADS):
        rq_ref[:, _head(h)] = _rope_full(z[:, _head(h)], rc, rs).astype(BF16)
    z = _z_group(hb, w_ref, 1)
    rck, rsk = rc * RET_K_SCALE, rs * RET_K_SCALE
    for h in range(N_HEADS):
        rk_ref[:, _head(h)] = _rope_full(z[:, _head(h)], rck, rsk).astype(BF16)
    rv_ref[...] = _z_group(hb, w_ref, 2).astype(BF16)
    rg_ref[...] = _z_group(hb, w_ref, 3).astype(BF16)

    z = _z_group(hb, w_ref, 4)
    dcq, daq, dbq = dc * DIFF_Q_SCALE, da * DIFF_Q_SCALE, db * DIFF_Q_SCALE
    for h in range(N_HEADS):
        qn = _map_rms(z[:, _head(h)], seg, gq_ref[...])
        dq_ref[:, _head(h)] = _rope_partial(qn, dcq, daq, dbq).astype(BF16)
    z = _z_group(hb, w_ref, 5)
    for h in range(N_HEADS):
        kn = _map_rms(z[:, _head(h)], seg, gk_ref[...])
        kr = _rope_partial(kn, dc, da, db)
        dk_ref[:, _head(h)] = kr
        dkb_ref[:, _head(h)] = kr.astype(BF16)
    z = _z_group(hb, w_ref, 6)
    tm = z.shape[0]
    ones = jnp.ones((ONES_ROWS, tm), BF16)
    for h in range(N_HEADS):
        zh = z[:, _head(h)]
        dv_ref[pl.ds(h, tm, stride=N_HEADS), :] = zh
        dvt_ref[h, :D_HEAD, :] = zh.T.astype(BF16)
        dvt_ref[h, D_HEAD:, :] = ones
    dg_ref[...] = _z_group(hb, w_ref, 7).astype(BF16)


def _proj_prompt(x, ng, w_bf, tabs, gq, gk, seg):
    B, L, _ = x.shape
    tm = TOKEN_TILE
    nt = L // tm
    tok = lambda width: pl.BlockSpec((None, tm, width), lambda b, i: (b, i, 0))
    tab = pl.BlockSpec((tm, LANES), lambda b, i: (i, 0))
    const = lambda shape: pl.BlockSpec(shape, lambda b, i: (0,) * len(shape))
    bf_out = jax.ShapeDtypeStruct((B, L, D_GROUP), BF16)
    f32_out = jax.ShapeDtypeStruct((B, L, D_GROUP), F32)
    return pl.pallas_call(
        _proj_prompt_kernel,
        grid=(B, nt),
        in_specs=[tok(D_MODEL), const((1, D_MODEL)), const((D_MODEL, 8 * D_GROUP)),
                  tab, tab, tab, tab, tab,
                  const((1, LANES)), const((1, LANES)), const((LANES, LANES))],
        out_specs=[tok(D_GROUP), tok(D_GROUP), tok(D_GROUP), tok(D_GROUP),
                   tok(D_GROUP), tok(D_GROUP), tok(D_GROUP),
                   pl.BlockSpec((None, tm * N_HEADS, D_HEAD), lambda b, i: (b, i, 0)),
                   pl.BlockSpec((None, None, N_HEADS, VT_ROWS, tm), lambda b, i: (b, i, 0, 0, 0)),
                   tok(D_GROUP)],
        out_shape=[bf_out, bf_out, bf_out, bf_out, bf_out, f32_out, bf_out,
                   jax.ShapeDtypeStruct((B, L * N_HEADS, D_HEAD), F32),
                   jax.ShapeDtypeStruct((B, nt, N_HEADS, VT_ROWS, tm), BF16), bf_out],
        compiler_params=pltpu.CompilerParams(
            dimension_semantics=("arbitrary", "arbitrary"), vmem_limit_bytes=VMEM_LIMIT_BYTES),
        name="proj_prompt",
    )(x, ng, w_bf, *tabs, gq, gk, seg)


def _ret_prompt_kernel(q_ref, k_ref, v_ref, g_ref, go_ref, u_ref, sfin_ref, s_scr, dm_scr, qd_scr, kd_scr):
    C = dm_scr.shape[-1]
    n_sub = q_ref.shape[0] // C
    b, c = pl.program_id(0), pl.program_id(1)

    @pl.when((b == 0) & (c == 0))
    def _():
        rel = (lax.broadcasted_iota(jnp.int32, (C, C), 0) - lax.broadcasted_iota(jnp.int32, (C, C), 1)).astype(F32)
        idx = lax.broadcasted_iota(jnp.int32, (C, D_HEAD), 0).astype(F32)
        for h in range(N_HEADS):
            lg = LOG_GAMMA[h]
            dm_scr[h] = jnp.where(rel >= 0, jnp.exp(lg * jnp.maximum(rel, 0.0)), 0.0)
            qd_scr[h] = jnp.exp(lg * (idx + 1.0))
            kd_scr[h] = jnp.exp(lg * (C - 1.0 - idx))

    @pl.when(c == 0)
    def _():
        s_scr[...] = jnp.zeros_like(s_scr)

    state = [s_scr[h] for h in range(N_HEADS)]
    for sub in range(n_sub):
        rows = slice(sub * C, (sub + 1) * C)
        for h in range(N_HEADS):
            lg = LOG_GAMMA[h]
            q, k, v = q_ref[rows, _head(h)], k_ref[rows, _head(h)], v_ref[rows, _head(h)]
            s_old = state[h]
            inner = lax.dot_general(q, k, _NT, preferred_element_type=F32) * dm_scr[h]
            q_dec = (q.astype(F32) * qd_scr[h]).astype(BF16)
            o = (jnp.dot(inner.astype(BF16), v, preferred_element_type=F32)
                 + jnp.dot(q_dec, s_old.astype(BF16), preferred_element_type=F32))
            k_dec = (k.astype(F32) * kd_scr[h]).astype(BF16)
            state[h] = math.exp(lg * C) * s_old + lax.dot_general(k_dec, v, _TN, preferred_element_type=F32)
            u_ref[rows, _head(h)] = (_rms_rows(o, go_ref[...])
                                     * _silu(g_ref[rows, _head(h)].astype(F32))).astype(BF16)
    for h in range(N_HEADS):
        s_scr[h] = state[h]

    @pl.when(c == pl.num_programs(1) - 1)
    def _():
        sfin_ref[...] = s_scr[...]


def _ret_prompt(rq, rk, rv, rg, go):
    B, L, _ = rq.shape
    C, T = RET_CHUNK, RET_TILE
    assert T % C == 0 and L % T == 0
    tok = pl.BlockSpec((None, T, D_GROUP), lambda b, c: (b, c, 0))
    return pl.pallas_call(
        _ret_prompt_kernel,
        grid=(B, L // T),
        in_specs=[tok, tok, tok, tok, pl.BlockSpec((1, D_HEAD), lambda b, c: (0, 0))],
        out_specs=[tok, pl.BlockSpec((None, N_HEADS, D_HEAD, D_HEAD), lambda b, c: (b, 0, 0, 0))],
        out_shape=[jax.ShapeDtypeStruct((B, L, D_GROUP), BF16),
                   jax.ShapeDtypeStruct((B, N_HEADS, D_HEAD, D_HEAD), F32)],
        scratch_shapes=[pltpu.VMEM((N_HEADS, D_HEAD, D_HEAD), F32), pltpu.VMEM((N_HEADS, C, C), F32),
                        pltpu.VMEM((N_HEADS, C, D_HEAD), F32), pltpu.VMEM((N_HEADS, C, D_HEAD), F32)],
        compiler_params=pltpu.CompilerParams(
            dimension_semantics=("arbitrary", "arbitrary"), vmem_limit_bytes=VMEM_LIMIT_BYTES),
        name="ret_prompt",
    )(rq, rk, rv, rg, go)


class _Decode:
    def __init__(self, pt_ref, qt_ref, q_ref, kn_ref, vn_ref, g_ref, go_ref, lam_ref, k_hbm, v_hbm, us_ref,
                 kbuf, vbuf, sem, qb_scr, acc_scr, m_scr, l_scr):
        self.__dict__.update(locals())
        n_seq, n_pages = pt_ref.shape
        self.groups_per_seq = n_pages // PAGES_PER_GROUP
        self.n_groups = n_seq * self.groups_per_seq
        self.n_maps = 2 * N_HEADS

    def _copies(self, gg):
        seq, first = lax.div(gg, self.groups_per_seq), lax.rem(gg, self.groups_per_seq) * PAGES_PER_GROUP
        slot = lax.rem(gg, DECODE_SLOTS)
        out = []
        for i in range(PAGES_PER_GROUP):
            page = self.pt_ref[seq, first + i]
            out.append(pltpu.make_async_copy(self.k_hbm.at[page], self.kbuf.at[slot, i], self.sem.at[slot]))
            out.append(pltpu.make_async_copy(self.v_hbm.at[page], self.vbuf.at[slot, i], self.sem.at[slot]))
        return out

    def start(self, gg):
        for cp in self._copies(gg):
            cp.start()

    def wait(self, gg):
        slot = lax.rem(gg, DECODE_SLOTS)
        for buf in (self.kbuf, self.vbuf):
            pltpu.make_async_copy(buf.at[slot], buf.at[slot], self.sem.at[slot]).wait()

    def begin_sequence(self, seq):
        lane = lax.broadcasted_iota(jnp.int32, (D_GROUP, LANES), 1)
        q_col = jnp.sum(jnp.where(lane == seq, self.qt_ref[...], 0.0), axis=1, keepdims=True)
        self.qb_scr[...] = jnp.broadcast_to(q_col, (D_GROUP, LANES))
        self.m_scr[...] = jnp.full(self.m_scr.shape, NEG_INF, F32)
        self.l_scr[...] = jnp.zeros_like(self.l_scr)
        self.acc_scr[...] = jnp.zeros_like(self.acc_scr)

    def group(self, slot):
        row_id = lax.broadcasted_iota(jnp.int32, (self.n_maps, PAGE_SIZE), 0)
        s_pages = [jnp.zeros((self.n_maps, PAGE_SIZE), F32)] * PAGES_PER_GROUP
        for r in range(self.n_maps):
            rows = slice(r * D_MAP, (r + 1) * D_MAP)
            q_r = self.qb_scr[rows, :]
            for i in range(PAGES_PER_GROUP):
                s_r = jnp.sum(self.kbuf[slot, i, rows, :] * q_r, axis=0, keepdims=True)
                s_pages[i] = jnp.where(row_id == r, s_r, s_pages[i])
        s = jnp.concatenate(s_pages, axis=1)
        m_old = self.m_scr[...]
        m_new = jnp.maximum(m_old, jnp.max(s, axis=1, keepdims=True))
        alpha = jnp.exp2(m_old - m_new)
        p = jnp.exp2(s - m_new)
        self.l_scr[...] = alpha * self.l_scr[...] + jnp.sum(p, axis=1, keepdims=True)
        p_bf = p.astype(BF16)
        pv = []
        for h in range(N_HEADS):
            v_h = jnp.concatenate([self.vbuf[slot, i, pl.ds(h, PAGE_SIZE, stride=N_HEADS), :]
                                   for i in range(PAGES_PER_GROUP)], axis=0)
            pv.append(jnp.dot(p_bf, v_h.astype(BF16), preferred_element_type=F32))
        self.acc_scr[...] = alpha * self.acc_scr[...] + jnp.concatenate(pv, axis=1)
        self.m_scr[...] = m_new

    def end_sequence(self, seq):
        n_maps = self.n_maps
        r_idx = lax.broadcasted_iota(jnp.int32, (n_maps, D_GROUP), 0)
        c_idx = lax.broadcasted_iota(jnp.int32, (n_maps, D_GROUP), 1)
        in_map = (c_idx >= r_idx * D_MAP) & (c_idx < (r_idx + 1) * D_MAP)
        q_bd = jnp.where(in_map, jnp.broadcast_to(self.q_ref[seq], (n_maps, D_GROUP)), 0.0)
        s_new = jnp.sum(q_bd * self.kn_ref[seq], axis=1, keepdims=True)
        m_old = self.m_scr[...]
        m_fin = jnp.maximum(m_old, s_new)
        alpha = jnp.exp2(m_old - m_fin)
        p_new = jnp.exp2(s_new - m_fin)
        l_fin = alpha * self.l_scr[...] + p_new
        acc = (alpha * self.acc_scr[...] + p_new * self.vn_ref[seq]) * (1.0 / l_fin)
        lam = _lambda(self.lam_ref)
        gate = self.g_ref[seq]
        for h in range(N_HEADS):
            o = acc[2 * h:2 * h + 1, _head(h)] - lam * acc[2 * h + 1:2 * h + 2, _head(h)]
            on = _rms_rows(o, self.go_ref[...]) * (1.0 - LAM_INIT)
            self.us_ref[seq, :, _head(h)] = on * _silu(gate[:, _head(h)])


def _diff_prompt_kernel(tq_tab, tk_tab, pt_ref, q_ref, k_ref, vt_ref, g_ref, go_ref, lam_ref,
                        qts_ref, qs_ref, kns_ref, vns_ref, gs_ref, k_hbm, v_hbm, u_ref, us_ref,
                        acc_scr, m_scr, sa_scr, mta_scr, sb_scr, mtb_scr,
                        kbuf, vbuf, dsem, qb_scr, dacc_scr, dm_scr, dl_scr):
    tq = tk = vt_ref.shape[-1]
    n_half = tq // Q_CHUNK
    n_steps = tq_tab.shape[0] - 1
    chunks = [(mp, hf) for mp in range(2) for hf in range(n_half)]
    lane = lax.broadcasted_iota(jnp.int32, (Q_CHUNK, D_HEAD), 1)
    in_map = (lane < D_MAP, lane >= D_MAP)
    dec = _Decode(pt_ref, qts_ref, qs_ref, kns_ref, vns_ref, gs_ref, go_ref, lam_ref, k_hbm, v_hbm, us_ref,
                  kbuf, vbuf, dsem, qb_scr, dacc_scr, dm_scr, dl_scr)
    first_g = (pl.program_id(0) * pl.num_programs(1) + pl.program_id(1)) * n_steps

    m_scr[...] = jnp.full(m_scr.shape, NEG_INF, F32)
    acc_scr[...] = jnp.zeros_like(acc_scr)

    def scores(f, s_buf, mt_buf):
        q0 = pl.multiple_of(tq_tab[f] * tq, tq)
        k = k_ref[pl.ds(pl.multiple_of(tk_tab[f] * tk, tk), tk), :]
        for c, (mp, hf) in enumerate(chunks):
            qc = q_ref[pl.ds(q0 + hf * Q_CHUNK, Q_CHUNK), :]
            qc = jnp.where(in_map[mp], qc, jnp.zeros_like(qc))
            s = lax.dot_general(k, qc, _NT, preferred_element_type=F32)
            s_buf[c] = s
            mt_buf[:, c * Q_CHUNK:(c + 1) * Q_CHUNK] = jnp.max(s, axis=0, keepdims=True)

    def consume(f, s_buf, mt_buf, diagonal):
        j = tk_tab[f]
        m_all, acc_all = m_scr[...], acc_scr[...]
        m_out, acc_out = [], []
        for c, (mp, hf) in enumerate(chunks):
            cols = slice(c * Q_CHUNK, (c + 1) * Q_CHUNK)
            m_old = m_all[:, cols]
            if diagonal:
                rows = (hf + 1) * Q_CHUNK
                s = s_buf[c, :rows, :]
                k_pos = lax.broadcasted_iota(jnp.int32, s.shape, 0)
                q_pos = hf * Q_CHUNK + lax.broadcasted_iota(jnp.int32, s.shape, 1)
                s = jnp.where(k_pos <= q_pos, s, NEG_INF)
                m_new = jnp.maximum(m_old, jnp.max(s, axis=0, keepdims=True))
            else:
                rows = tk
                s = s_buf[c]
                m_new = jnp.maximum(m_old, mt_buf[:, cols])
            alpha = jnp.exp2(m_old - m_new)
            p = jnp.exp2((s - m_new).astype(BF16))
            pv = jnp.dot(vt_ref[j, :, :rows], p, preferred_element_type=F32)
            acc_out.append(alpha * acc_all[:, cols] + pv)
            m_out.append(m_new)
        if not diagonal:
            m_scr[...] = jnp.concatenate(m_out, axis=1)
            acc_scr[...] = jnp.concatenate(acc_out, axis=1)
            return
        acc = jnp.concatenate(acc_out, axis=1)
        o_all = acc[:D_HEAD] * (1.0 / acc[D_HEAD:D_HEAD + 1])
        o_t = o_all[:, :tq] - _lambda(lam_ref) * o_all[:, tq:]
        on = _rms_rows(o_t.T, go_ref[...]) * (1.0 - LAM_INIT)
        q_rows = pl.ds(pl.multiple_of(j * tq, tq), tq)
        u_ref[q_rows, :] = (on * _silu(g_ref[q_rows, :].astype(F32))).astype(BF16)
        m_scr[...] = jnp.full(m_scr.shape, NEG_INF, F32)
        acc_scr[...] = jnp.zeros_like(acc_scr)

    def step(f, cur, nxt):
        g = first_g + f
        seq, grp = lax.div(g, dec.groups_per_seq), lax.rem(g, dec.groups_per_seq)
        live = g < dec.n_groups

        @pl.when(g + 2 < dec.n_groups)
        def _():
            dec.start(g + 2)

        @pl.when(live)
        def _():
            dec.wait(g)

        @pl.when(live & (grp == 0))
        def _():
            dec.begin_sequence(seq)

        diagonal = tq_tab[f] == tk_tab[f]
        slot = lax.rem(g, DECODE_SLOTS)

        @pl.when(diagonal)
        def _():
            scores(f + 1, *nxt)
            consume(f, *cur, diagonal=True)
            dec.group(slot)

        @pl.when(jnp.logical_not(diagonal))
        def _():
            scores(f + 1, *nxt)
            consume(f, *cur, diagonal=False)
            dec.group(slot)

        @pl.when(live & (grp == dec.groups_per_seq - 1))
        def _():
            dec.end_sequence(seq)

    @pl.when(first_g == 0)
    def _():
        dec.start(0)
        dec.start(1)

    buf_a, buf_b = (sa_scr, mta_scr), (sb_scr, mtb_scr)
    scores(0, *buf_a)

    def pair(i, carry):
        step(2 * i, buf_a, buf_b)
        step(2 * i + 1, buf_b, buf_a)
        return carry

    assert n_steps % 2 == 0
    lax.fori_loop(0, n_steps // 2, pair, 0)


def _diff_attention(dq, dkb, dvt, dg, go, lam_rows, page_table, dqt_s, dq_s, dk_s, dv_s, dg_s, k_pages, v_pages):
    B, L, _ = dq.shape
    nk, tk = dvt.shape[1], dvt.shape[4]
    n_s, n_pages = page_table.shape
    assert L == nk * tk and tk % Q_CHUNK == 0 and n_pages % PAGES_PER_GROUP == 0
    pairs = [(qi, j) for qi in range(nk) for j in range(qi + 1)] + [(0, 0)]
    n_steps = len(pairs) - 1
    assert B * N_HEADS * n_steps >= n_s * (n_pages // PAGES_PER_GROUP) >= DECODE_SLOTS
    tq_tab = jnp.asarray([p[0] for p in pairs], jnp.int32)
    tk_tab = jnp.asarray([p[1] for p in pairs], jnp.int32)
    seq = pl.BlockSpec((None, L, D_HEAD), lambda b, h, *_: (b, 0, h))
    whole = lambda shape: pl.BlockSpec(shape, lambda b, h, *_: (0,) * len(shape))
    rows_s = whole((n_s, 1, D_GROUP))
    hbm = pl.BlockSpec(memory_space=pl.ANY)
    n_maps = 2 * N_HEADS
    group_buf = pltpu.VMEM((DECODE_SLOTS, PAGES_PER_GROUP, D_GROUP, LANES), F32)
    grid_spec = pltpu.PrefetchScalarGridSpec(
        num_scalar_prefetch=3,
        grid=(B, N_HEADS),
        in_specs=[seq, seq,
                  pl.BlockSpec((None, nk, None, VT_ROWS, tk), lambda b, h, *_: (b, 0, h, 0, 0)),
                  seq, whole((1, D_HEAD)), whole((4, D_MAP)),
                  whole((D_GROUP, LANES)), rows_s, rows_s, rows_s, rows_s, hbm, hbm],
        out_specs=[seq, rows_s],
        scratch_shapes=[pltpu.VMEM((VT_ROWS, 2 * tk), F32), pltpu.VMEM((1, 2 * tk), F32)]
                       + [pltpu.VMEM((2 * tk // Q_CHUNK, tk, Q_CHUNK), F32), pltpu.VMEM((1, 2 * tk), F32)] * 2
                       + [group_buf, group_buf, pltpu.SemaphoreType.DMA((DECODE_SLOTS,)),
                          pltpu.VMEM((D_GROUP, LANES), F32), pltpu.VMEM((n_maps, D_GROUP), F32),
                          pltpu.VMEM((n_maps, 1), F32), pltpu.VMEM((n_maps, 1), F32)],
    )
    return pl.pallas_call(
        _diff_prompt_kernel,
        grid_spec=grid_spec,
        out_shape=[jax.ShapeDtypeStruct((B, L, D_GROUP), BF16), jax.ShapeDtypeStruct((n_s, 1, D_GROUP), F32)],
        compiler_params=pltpu.CompilerParams(
            dimension_semantics=("arbitrary", "arbitrary"), vmem_limit_bytes=VMEM_LIMIT_BYTES),
        name="diff_attention",
    )(tq_tab, tk_tab, page_table, dq, dkb, dvt, dg, go, lam_rows, dqt_s, dq_s, dk_s, dv_s, dg_s, k_pages, v_pages)


def _out_proj_kernel(x_ref, ur_ref, ud_ref, w_ref, y_ref):
    y_ref[...] = (x_ref[...]
                  + jnp.dot(ur_ref[...].astype(BF16), w_ref[:D_GROUP, :], preferred_element_type=F32)
                  + jnp.dot(ud_ref[...].astype(BF16), w_ref[D_GROUP:, :], preferred_element_type=F32))


def _out_proj(x2d, ur, ud, w_bf, tm):
    T = x2d.shape[0]
    tok = lambda width: pl.BlockSpec((tm, width), lambda i: (i, 0))
    return pl.pallas_call(
        _out_proj_kernel,
        grid=(T // tm,),
        in_specs=[tok(D_MODEL), tok(D_GROUP), tok(D_GROUP),
                  pl.BlockSpec((2 * D_GROUP, D_MODEL), lambda i: (0, 0))],
        out_specs=tok(D_MODEL),
        out_shape=jax.ShapeDtypeStruct((T, D_MODEL), F32),
        compiler_params=pltpu.CompilerParams(
            dimension_semantics=("arbitrary",), vmem_limit_bytes=VMEM_LIMIT_BYTES),
        name="out_proj",
    )(x2d, ur, ud, w_bf)


def _proj_sample_kernel(x_ref, ng_ref, w_ref, rc_ref, rs_ref, dc_ref, da_ref, db_ref, gq_ref, gk_ref, seg_ref,
                        rqt_ref, rkt_ref, dqt_ref, rv_ref, rg_ref, dq_ref, dk_ref, dv_ref, dg_ref):
    hb = _normed_input(x_ref, ng_ref)
    n = hb.shape[0]
    rc, rs = rc_ref[...], rs_ref[...]
    dc, da, db = dc_ref[...], da_ref[...], db_ref[...]
    seg = seg_ref[...]
    pad = jnp.zeros((LANES - n, D_HEAD), F32)

    def transposed(xh):
        return jnp.concatenate([xh, pad], axis=0).T

    z = _z_group(hb, w_ref, 0)
    for h in range(N_HEADS):
        rqt_ref[_head(h), :] = transposed(_rope_full(z[:, _head(h)], rc, rs))
    z = _z_group(hb, w_ref, 1)
    rck, rsk = rc * RET_K_SCALE, rs * RET_K_SCALE
    for h in range(N_HEADS):
        rkt_ref[_head(h), :] = transposed(_rope_full(z[:, _head(h)], rck, rsk))
    rv_ref[...] = _z_group(hb, w_ref, 2)
    rg_ref[...] = _z_group(hb, w_ref, 3)
    z = _z_group(hb, w_ref, 4)
    dcq, daq, dbq = dc * DIFF_Q_SCALE, da * DIFF_Q_SCALE, db * DIFF_Q_SCALE
    for h in range(N_HEADS):
        qh = _rope_partial(_map_rms(z[:, _head(h)], seg, gq_ref[...]), dcq, daq, dbq)
        dq_ref[:, _head(h)] = qh
        dqt_ref[_head(h), :] = transposed(qh)
    z = _z_group(hb, w_ref, 5)
    for h in range(N_HEADS):
        dk_ref[:, _head(h)] = _rope_partial(_map_rms(z[:, _head(h)], seg, gk_ref[...]), dc, da, db)
    dv_ref[...] = _z_group(hb, w_ref, 6)
    dg_ref[...] = _z_group(hb, w_ref, 7)


def _proj_sample(x, ng, w_bf, tabs, gq, gk, seg):
    n = x.shape[0]
    full = lambda shape: pl.BlockSpec(shape, lambda i: (0,) * len(shape))
    row_out = jax.ShapeDtypeStruct((n, D_GROUP), F32)
    col_out = jax.ShapeDtypeStruct((D_GROUP, LANES), F32)
    return pl.pallas_call(
        _proj_sample_kernel,
        grid=(1,),
        in_specs=[full((n, D_MODEL)), full((1, D_MODEL)), full((D_MODEL, 8 * D_GROUP))]
                 + [full((n, LANES))] * 5 + [full((1, LANES)), full((1, LANES)), full((LANES, LANES))],
        out_specs=[full((D_GROUP, LANES))] * 3 + [full((n, D_GROUP))] * 6,
        out_shape=[col_out] * 3 + [row_out] * 6,
        compiler_params=pltpu.CompilerParams(
            dimension_semantics=("arbitrary",), vmem_limit_bytes=VMEM_LIMIT_BYTES),
        name="proj_sample",
    )(x, ng, w_bf, *tabs, gq, gk, seg)


def _ret_sample_kernel(qt_ref, kt_ref, v_ref, g_ref, s_ref, go_ref, u_ref, snew_ref):
    per_step = v_ref.shape[0]
    lane = lax.broadcasted_iota(jnp.int32, (D_HEAD, LANES), 1)

    def column(ref, h, b):
        return jnp.sum(jnp.where(lane == b, ref[_head(h), :], 0.0), axis=1, keepdims=True)

    for i in range(per_step):
        b = pl.program_id(0) * per_step + i
        for h in range(N_HEADS):
            gamma = math.exp(LOG_GAMMA[h])
            q_col, k_col = column(qt_ref, h, b), column(kt_ref, h, b)
            v_row = v_ref[i, :, _head(h)]
            s_old = s_ref[i, h]
            inner = jnp.sum(q_col * k_col, axis=0, keepdims=True)
            o = inner * v_row + jnp.sum((q_col * gamma) * s_old, axis=0, keepdims=True)
            snew_ref[i, h] = gamma * s_old + k_col * v_row
            u_ref[i, :, _head(h)] = _rms_rows(o, go_ref[...]) * _silu(g_ref[i, :, _head(h)])


def _ret_sample(rqt, rkt, rv3, rg3, state, go):
    n = state.shape[0]
    per_step = RET_SAMPLE_SEQS
    assert n % per_step == 0
    col = pl.BlockSpec((D_GROUP, LANES), lambda b: (0, 0))
    row = pl.BlockSpec((per_step, 1, D_GROUP), lambda b: (b, 0, 0))
    st = pl.BlockSpec((per_step, N_HEADS, D_HEAD, D_HEAD), lambda b: (b, 0, 0, 0))
    return pl.pallas_call(
        _ret_sample_kernel,
        grid=(n // per_step,),
        in_specs=[col, col, row, row, st, pl.BlockSpec((1, D_HEAD), lambda b: (0, 0))],
        out_specs=[row, st],
        out_shape=[jax.ShapeDtypeStruct((n, 1, D_GROUP), F32),
                   jax.ShapeDtypeStruct((n, N_HEADS, D_HEAD, D_HEAD), F32)],
        compiler_params=pltpu.CompilerParams(dimension_semantics=("arbitrary",)),
        name="ret_sample",
    )(rqt, rkt, rv3, rg3, state, go)


def _rope_tables(pos):
    posf = np.asarray(pos, np.float64)[:, None]
    fr = RET_ROPE_THETA ** (-np.arange(0, D_HEAD, 2, dtype=np.float64) / D_HEAD)
    ang = posf * fr[None, :]
    cos, sin = np.cos(ang), np.sin(ang)
    rc = np.concatenate([cos, cos], axis=-1)
    rs = np.concatenate([-sin, sin], axis=-1)
    half = N_ROT_DIFF // 2
    fd = ROPE_THETA ** (-np.arange(0, N_ROT_DIFF, 2, dtype=np.float64) / N_ROT_DIFF)
    angd = posf * fd[None, :]
    cd, sd = np.cos(angd), np.sin(angd)
    n = posf.shape[0]
    ones = np.ones((n, D_MAP - N_ROT_DIFF))
    zeros = np.zeros((n, D_MAP - N_ROT_DIFF))
    z8 = np.zeros((n, half))
    dc = np.concatenate([cd, cd, ones], axis=-1)
    da = np.concatenate([-sd, z8, zeros], axis=-1)
    db = np.concatenate([z8, sd, zeros], axis=-1)
    twice = lambda t: np.concatenate([t, t], axis=-1)
    return tuple(jnp.asarray(t, F32) for t in (rc, rs, twice(dc), twice(da), twice(db)))


def kernel(x_prompt, x_sample, cache_k, cache_v, state_ret, page_table, norm_g, w_in, w_out,
           q_norm_g, k_norm_g, ret_out_g, diff_out_g, lambda_q1, lambda_k1, lambda_q2, lambda_k2):
    B, L, _ = x_prompt.shape
    n_s = x_sample.shape[0]
    n_pages = page_table.shape[1]
    past = n_pages * PAGE_SIZE
    l = 0

    ng = norm_g[l][None, :]
    w_in_bf = w_in[l].astype(BF16)
    w_out_bf = w_out[l].astype(BF16)
    gq = jnp.tile(q_norm_g[l], 2)[None, :]
    gk = jnp.tile(k_norm_g[l], 2)[None, :]
    go_ret = ret_out_g[l][None, :]
    go_diff = diff_out_g[l][None, :]
    lam_rows = jnp.stack([lambda_q1[l], lambda_k1[l], lambda_q2[l], lambda_k2[l]])
    lane = np.arange(LANES)
    seg = jnp.asarray(np.where((lane[:, None] // D_MAP) == (lane[None, :] // D_MAP), 1.0 / D_MAP, 0.0), BF16)

    tabs_p = _rope_tables(np.arange(L))
    rq, rk, rv, rg, dq, dk, dkb, dv, dvt, dg = _proj_prompt(x_prompt, ng, w_in_bf, tabs_p, gq, gk, seg)
    tabs_s = _rope_tables(np.full((n_s,), past))
    rqt, rkt, dqt, rv_s, rg_s, dq_s, dk_s, dv_s, dg_s = _proj_sample(
        x_sample.reshape(n_s, D_MODEL), ng, w_in_bf, tabs_s, gq, gk, seg)
    as_rows = lambda t: t.reshape(n_s, 1, D_GROUP)

    u_ret, s_fin = _ret_prompt(rq, rk, rv, rg, go_ret)
    u_ret_s, s_new = _ret_sample(rqt, rkt, as_rows(rv_s), as_rows(rg_s), state_ret[l], go_ret)

    n_phys = cache_k.shape[1]
    k_pages = jnp.transpose(cache_k[l], (0, 2, 3, 4, 1)).reshape(n_phys, D_GROUP, PAGE_SIZE)
    v_pages = cache_v[l].reshape(n_phys, PAGE_SIZE * N_HEADS, D_HEAD)
    u_diff, u_diff_s = _diff_attention(dq, dkb, dvt, dg, go_diff, lam_rows, page_table, dqt, as_rows(dq_s),
                                       as_rows(dk_s), as_rows(dv_s), as_rows(dg_s), k_pages, v_pages)

    y_prompt = _out_proj(x_prompt.reshape(B * L, D_MODEL), u_ret.reshape(B * L, D_GROUP),
                         u_diff.reshape(B * L, D_GROUP), w_out_bf, OUT_TILE).reshape(B, L, D_MODEL)
    y_sample = _out_proj(x_sample.reshape(n_s, D_MODEL), u_ret_s.reshape(n_s, D_GROUP),
                         u_diff_s.reshape(n_s, D_GROUP), w_out_bf, n_s).reshape(n_s, 1, D_MODEL)

    return (y_prompt, y_sample,
            dk.reshape(1, B, L, N_HEADS, 2, D_MAP), dv.reshape(1, B, L, N_HEADS, D_HEAD), s_fin[None],
            dk_s.reshape(1, n_s, 1, N_HEADS, 2, D_MAP), dv_s.reshape(1, n_s, 1, N_HEADS, D_HEAD), s_new[None])
```

```python
import math

import jax
import jax.numpy as jnp
import numpy as np
from jax import lax
from jax.experimental import pallas as pl
from jax.experimental.pallas import tpu as pltpu

F32 = jnp.float32
BF16 = jnp.bfloat16

D_MODEL = 1024
N_HEADS = 4
D_HEAD = 128
D_GROUP = N_HEADS * D_HEAD
D_MAP = 64
N_ROT_DIFF = 16
ROPE_THETA = 500000.0
RET_ROPE_THETA = 10000.0
PAGE_SIZE = 128
EPS = 1e-6
NEG_INF = -1e30
LAM_INIT = 0.8 - 0.6 * math.exp(-0.3 * 0)
RET_K_SCALE = D_HEAD ** -0.5
DIFF_Q_SCALE = D_MAP ** -0.5 * math.log2(math.e)
LOG_GAMMA = tuple(math.log(1.0 - 2.0 ** (-5.0 - h)) for h in range(N_HEADS))

LANES = 128
VMEM_LIMIT_BYTES = 56 * 1024 * 1024

TOKEN_TILE = 512
Q_CHUNK = 256
ONES_ROWS = 16
VT_ROWS = D_HEAD + ONES_ROWS
RET_CHUNK = 256
RET_TILE = 1024
OUT_TILE = 1024
RET_SAMPLE_SEQS = 8
PAGES_PER_GROUP = 4
DECODE_SLOTS = 4

_NT = (((1,), (1,)), ((), ()))
_TN = (((0,), (0,)), ((), ()))


def _head(h):
    return slice(h * D_HEAD, (h + 1) * D_HEAD)


def _silu(g):
    return g * (1.0 / (1.0 + jnp.exp(-g)))


def _rms_rows(x, gain):
    return x * lax.rsqrt(jnp.mean(x * x, axis=-1, keepdims=True) + EPS) * gain


def _rope_full(x, cos_t, sin_t):
    return x * cos_t + pltpu.roll(x, D_HEAD // 2, 1) * sin_t


def _rope_partial(x, c_t, a_t, b_t):
    half = N_ROT_DIFF // 2
    return x * c_t + pltpu.roll(x, LANES - half, 1) * a_t + pltpu.roll(x, half, 1) * b_t


def _map_rms(x, seg, gain):
    ms = jnp.dot((x * x).astype(BF16), seg, preferred_element_type=F32)
    return x * lax.rsqrt(ms + EPS) * gain


def _lambda(lam_ref):
    l = lam_ref[...]
    s1 = jnp.sum(l[0:1] * l[1:2], axis=-1, keepdims=True)
    s2 = jnp.sum(l[2:3] * l[3:4], axis=-1, keepdims=True)
    return jnp.exp(s1) - jnp.exp(s2) + LAM_INIT


def _normed_input(x_ref, ng_ref):
    x = x_ref[...]
    return _rms_rows(x, ng_ref[...]).astype(BF16)


def _z_group(hb, w_ref, g):
    return jnp.dot(hb, w_ref[:, g * D_GROUP:(g + 1) * D_GROUP], preferred_element_type=F32)


def _proj_prompt_kernel(x_ref, ng_ref, w_ref, rc_ref, rs_ref, dc_ref, da_ref, db_ref, gq_ref, gk_ref, seg_ref,
                        rq_ref, rk_ref, rv_ref, rg_ref, dq_ref, dk_ref, dkb_ref, dv_ref, dvt_ref, dg_ref):
    hb = _normed_input(x_ref, ng_ref)
    rc, rs = rc_ref[...], rs_ref[...]
    dc, da, db = dc_ref[...], da_ref[...], db_ref[...]
    seg = seg_ref[...]

    z = _z_group(hb, w_ref, 0)
    for h in range(N_HEADS):
        rq_ref[:, _head(h)] = _rope_full(z[:, _head(h)], rc, rs).astype(BF16)
    z = _z_group(hb, w_ref, 1)
    rck, rsk = rc * RET_K_SCALE, rs * RET_K_SCALE
    for h in range(N_HEADS):
        rk_ref[:, _head(h)] = _rope_full(z[:, _head(h)], rck, rsk).astype(BF16)
    rv_ref[...] = _z_group(hb, w_ref, 2).astype(BF16)
    rg_ref[...] = _z_group(hb, w_ref, 3).astype(BF16)

    z = _z_group(hb, w_ref, 4)
    dcq, daq, dbq = dc * DIFF_Q_SCALE, da * DIFF_Q_SCALE, db * DIFF_Q_SCALE
    for h in range(N_HEADS):
        qn = _map_rms(z[:, _head(h)], seg, gq_ref[...])
        dq_ref[:, _head(h)] = _rope_partial(qn, dcq, daq, dbq).astype(BF16)
    z = _z_group(hb, w_ref, 5)
    for h in range(N_HEADS):
        kn = _map_rms(z[:, _head(h)], seg, gk_ref[...])
        kr = _rope_partial(kn, dc, da, db)
        dk_ref[:, _head(h)] = kr
        dkb_ref[:, _head(h)] = kr.astype(BF16)
    z = _z_group(hb, w_ref, 6)
    tm = z.shape[0]
    ones = jnp.ones((ONES_ROWS, tm), BF16)
    for h in range(N_HEADS):
        zh = z[:, _head(h)]
        dv_ref[pl.ds(h, tm, stride=N_HEADS), :] = zh
        dvt_ref[h, :D_HEAD, :] = zh.T.astype(BF16)
        dvt_ref[h, D_HEAD:, :] = ones
    dg_ref[...] = _z_group(hb, w_ref, 7).astype(BF16)


def _proj_prompt(x, ng, w_bf, tabs, gq, gk, seg):
    B, L, _ = x.shape
    tm = TOKEN_TILE
    nt = L // tm
    tok = lambda width: pl.BlockSpec((None, tm, width), lambda b, i: (b, i, 0))
    tab = pl.BlockSpec((tm, LANES), lambda b, i: (i, 0))
    const = lambda shape: pl.BlockSpec(shape, lambda b, i: (0,) * len(shape))
    bf_out = jax.ShapeDtypeStruct((B, L, D_GROUP), BF16)
    f32_out = jax.ShapeDtypeStruct((B, L, D_GROUP), F32)
    return pl.pallas_call(
        _proj_prompt_kernel,
        grid=(B, nt),
        in_specs=[tok(D_MODEL), const((1, D_MODEL)), const((D_MODEL, 8 * D_GROUP)),
                  tab, tab, tab, tab, tab,
                  const((1, LANES)), const((1, LANES)), const((LANES, LANES))],
        out_specs=[tok(D_GROUP), tok(D_GROUP), tok(D_GROUP), tok(D_GROUP),
                   tok(D_GROUP), tok(D_GROUP), tok(D_GROUP),
                   pl.BlockSpec((None, tm * N_HEADS, D_HEAD), lambda b, i: (b, i, 0)),
                   pl.BlockSpec((None, None, N_HEADS, VT_ROWS, tm), lambda b, i: (b, i, 0, 0, 0)),
                   tok(D_GROUP)],
        out_shape=[bf_out, bf_out, bf_out, bf_out, bf_out, f32_out, bf_out,
                   jax.ShapeDtypeStruct((B, L * N_HEADS, D_HEAD), F32),
                   jax.ShapeDtypeStruct((B, nt, N_HEADS, VT_ROWS, tm), BF16), bf_out],
        compiler_params=pltpu.CompilerParams(
            dimension_semantics=("arbitrary", "arbitrary"), vmem_limit_bytes=VMEM_LIMIT_BYTES),
        name="proj_prompt",
    )(x, ng, w_bf, *tabs, gq, gk, seg)


def _ret_prompt_kernel(q_ref, k_ref, v_ref, g_ref, go_ref, u_ref, sfin_ref, s_scr, dm_scr, qd_scr, kd_scr):
    C = dm_scr.shape[-1]
    n_sub = q_ref.shape[0] // C
    b, c = pl.program_id(0), pl.program_id(1)

    @pl.when((b == 0) & (c == 0))
    def _():
        rel = (lax.broadcasted_iota(jnp.int32, (C, C), 0) - lax.broadcasted_iota(jnp.int32, (C, C), 1)).astype(F32)
        idx = lax.broadcasted_iota(jnp.int32, (C, D_HEAD), 0).astype(F32)
        for h in range(N_HEADS):
            lg = LOG_GAMMA[h]
            dm_scr[h] = jnp.where(rel >= 0, jnp.exp(lg * jnp.maximum(rel, 0.0)), 0.0)
            qd_scr[h] = jnp.exp(lg * (idx + 1.0))
            kd_scr[h] = jnp.exp(lg * (C - 1.0 - idx))

    @pl.when(c == 0)
    def _():
        s_scr[...] = jnp.zeros_like(s_scr)

    state = [s_scr[h] for h in range(N_HEADS)]
    for sub in range(n_sub):
        rows = slice(sub * C, (sub + 1) * C)
        for h in range(N_HEADS):
            lg = LOG_GAMMA[h]
            q, k, v = q_ref[rows, _head(h)], k_ref[rows, _head(h)], v_ref[rows, _head(h)]
            s_old = state[h]
            inner = lax.dot_general(q, k, _NT, preferred_element_type=F32) * dm_scr[h]
            q_dec = (q.astype(F32) * qd_scr[h]).astype(BF16)
            o = (jnp.dot(inner.astype(BF16), v, preferred_element_type=F32)
                 + jnp.dot(q_dec, s_old.astype(BF16), preferred_element_type=F32))
            k_dec = (k.astype(F32) * kd_scr[h]).astype(BF16)
            state[h] = math.exp(lg * C) * s_old + lax.dot_general(k_dec, v, _TN, preferred_element_type=F32)
            u_ref[rows, _head(h)] = (_rms_rows(o, go_ref[...])
                                     * _silu(g_ref[rows, _head(h)].astype(F32))).astype(BF16)
    for h in range(N_HEADS):
        s_scr[h] = state[h]

    @pl.when(c == pl.num_programs(1) - 1)
    def _():
        sfin_ref[...] = s_scr[...]


def _ret_prompt(rq, rk, rv, rg, go):
    B, L, _ = rq.shape
    C, T = RET_CHUNK, RET_TILE
    assert T % C == 0 and L % T == 0
    tok = pl.BlockSpec((None, T, D_GROUP), lambda b, c: (b, c, 0))
    return pl.pallas_call(
        _ret_prompt_kernel,
        grid=(B, L // T),
        in_specs=[tok, tok, tok, tok, pl.BlockSpec((1, D_HEAD), lambda b, c: (0, 0))],
        out_specs=[tok, pl.BlockSpec((None, N_HEADS, D_HEAD, D_HEAD), lambda b, c: (b, 0, 0, 0))],
        out_shape=[jax.ShapeDtypeStruct((B, L, D_GROUP), BF16),
                   jax.ShapeDtypeStruct((B, N_HEADS, D_HEAD, D_HEAD), F32)],
        scratch_shapes=[pltpu.VMEM((N_HEADS, D_HEAD, D_HEAD), F32), pltpu.VMEM((N_HEADS, C, C), F32),
                        pltpu.VMEM((N_HEADS, C, D_HEAD), F32), pltpu.VMEM((N_HEADS, C, D_HEAD), F32)],
        compiler_params=pltpu.CompilerParams(
            dimension_semantics=("arbitrary", "arbitrary"), vmem_limit_bytes=VMEM_LIMIT_BYTES),
        name="ret_prompt",
    )(rq, rk, rv, rg, go)


class _Decode:
    def __init__(self, pt_ref, qt_ref, q_ref, kn_ref, vn_ref, g_ref, go_ref, lam_ref, k_hbm, v_hbm, us_ref,
                 kbuf, vbuf, sem, qb_scr, acc_scr, m_scr, l_scr):
        self.pt_ref, self.qt_ref, self.q_ref, self.kn_ref, self.vn_ref = pt_ref, qt_ref, q_ref, kn_ref, vn_ref
        self.g_ref, self.go_ref, self.lam_ref, self.us_ref = g_ref, go_ref, lam_ref, us_ref
        self.k_hbm, self.v_hbm, self.kbuf, self.vbuf, self.sem = k_hbm, v_hbm, kbuf, vbuf, sem
        self.qb_scr, self.acc_scr, self.m_scr, self.l_scr = qb_scr, acc_scr, m_scr, l_scr
        n_seq, n_pages = pt_ref.shape
        self.groups_per_seq = n_pages // PAGES_PER_GROUP
        self.n_groups = n_seq * self.groups_per_seq
        self.n_maps = 2 * N_HEADS

    def _copies(self, gg):
        seq, first = lax.div(gg, self.groups_per_seq), lax.rem(gg, self.groups_per_seq) * PAGES_PER_GROUP
        slot = lax.rem(gg, DECODE_SLOTS)
        out = []
        for i in range(PAGES_PER_GROUP):
            page = self.pt_ref[seq, first + i]
            out.append(pltpu.make_async_copy(self.k_hbm.at[page], self.kbuf.at[slot, i], self.sem.at[slot]))
            out.append(pltpu.make_async_copy(self.v_hbm.at[page], self.vbuf.at[slot, i], self.sem.at[slot]))
        return out

    def start(self, gg):
        for cp in self._copies(gg):
            cp.start()

    def wait(self, gg):
        slot = lax.rem(gg, DECODE_SLOTS)
        for buf in (self.kbuf, self.vbuf):
            pltpu.make_async_copy(buf.at[slot], buf.at[slot], self.sem.at[slot]).wait()

    def begin_sequence(self, seq):
        lane = lax.broadcasted_iota(jnp.int32, (D_GROUP, LANES), 1)
        q_col = jnp.sum(jnp.where(lane == seq, self.qt_ref[...], 0.0), axis=1, keepdims=True)
        self.qb_scr[...] = jnp.broadcast_to(q_col, (D_GROUP, LANES))
        self.m_scr[...] = jnp.full(self.m_scr.shape, NEG_INF, F32)
        self.l_scr[...] = jnp.zeros_like(self.l_scr)
        self.acc_scr[...] = jnp.zeros_like(self.acc_scr)

    def group(self, slot):
        row_id = lax.broadcasted_iota(jnp.int32, (self.n_maps, PAGE_SIZE), 0)
        s_pages = [jnp.zeros((self.n_maps, PAGE_SIZE), F32)] * PAGES_PER_GROUP
        for r in range(self.n_maps):
            rows = slice(r * D_MAP, (r + 1) * D_MAP)
            q_r = self.qb_scr[rows, :]
            for i in range(PAGES_PER_GROUP):
                s_r = jnp.sum(self.kbuf[slot, i, rows, :] * q_r, axis=0, keepdims=True)
                s_pages[i] = jnp.where(row_id == r, s_r, s_pages[i])
        s = jnp.concatenate(s_pages, axis=1)
        m_old = self.m_scr[...]
        m_new = jnp.maximum(m_old, jnp.max(s, axis=1, keepdims=True))
        alpha = jnp.exp2(m_old - m_new)
        p = jnp.exp2(s - m_new)
        self.l_scr[...] = alpha * self.l_scr[...] + jnp.sum(p, axis=1, keepdims=True)
        p_bf = p.astype(BF16)
        pv = []
        for h in range(N_HEADS):
            v_h = jnp.concatenate([self.vbuf[slot, i, pl.ds(h, PAGE_SIZE, stride=N_HEADS), :]
                                   for i in range(PAGES_PER_GROUP)], axis=0)
            pv.append(jnp.dot(p_bf, v_h.astype(BF16), preferred_element_type=F32))
        self.acc_scr[...] = alpha * self.acc_scr[...] + jnp.concatenate(pv, axis=1)
        self.m_scr[...] = m_new

    def end_sequence(self, seq):
        n_maps = self.n_maps
        r_idx = lax.broadcasted_iota(jnp.int32, (n_maps, D_GROUP), 0)
        c_idx = lax.broadcasted_iota(jnp.int32, (n_maps, D_GROUP), 1)
        in_map = (c_idx >= r_idx * D_MAP) & (c_idx < (r_idx + 1) * D_MAP)
        q_bd = jnp.where(in_map, jnp.broadcast_to(self.q_ref[seq], (n_maps, D_GROUP)), 0.0)
        s_new = jnp.sum(q_bd * self.kn_ref[seq], axis=1, keepdims=True)
        m_old = self.m_scr[...]
        m_fin = jnp.maximum(m_old, s_new)
        alpha = jnp.exp2(m_old - m_fin)
        p_new = jnp.exp2(s_new - m_fin)
        l_fin = alpha * self.l_scr[...] + p_new
        acc = (alpha * self.acc_scr[...] + p_new * self.vn_ref[seq]) * (1.0 / l_fin)
        lam = _lambda(self.lam_ref)
        gate = self.g_ref[seq]
        for h in range(N_HEADS):
            o = acc[2 * h:2 * h + 1, _head(h)] - lam * acc[2 * h + 1:2 * h + 2, _head(h)]
            on = _rms_rows(o, self.go_ref[...]) * (1.0 - LAM_INIT)
            self.us_ref[seq, :, _head(h)] = on * _silu(gate[:, _head(h)])


def _diff_prompt_kernel(tq_tab, tk_tab, pt_ref, q_ref, k_ref, vt_ref, g_ref, go_ref, lam_ref,
                        qts_ref, qs_ref, kns_ref, vns_ref, gs_ref, k_hbm, v_hbm, u_ref, us_ref,
                        acc_scr, m_scr, sa_scr, mta_scr, sb_scr, mtb_scr,
                        kbuf, vbuf, dsem, qb_scr, dacc_scr, dm_scr, dl_scr):
    tq = tk = vt_ref.shape[-1]
    n_half = tq // Q_CHUNK
    n_steps = tq_tab.shape[0] - 1
    chunks = [(mp, hf) for mp in range(2) for hf in range(n_half)]
    lane = lax.broadcasted_iota(jnp.int32, (Q_CHUNK, D_HEAD), 1)
    in_map = (lane < D_MAP, lane >= D_MAP)
    dec = _Decode(pt_ref, qts_ref, qs_ref, kns_ref, vns_ref, gs_ref, go_ref, lam_ref, k_hbm, v_hbm, us_ref,
                  kbuf, vbuf, dsem, qb_scr, dacc_scr, dm_scr, dl_scr)
    first_g = (pl.program_id(0) * pl.num_programs(1) + pl.program_id(1)) * n_steps

    m_scr[...] = jnp.full(m_scr.shape, NEG_INF, F32)
    acc_scr[...] = jnp.zeros_like(acc_scr)

    def scores(f, s_buf, mt_buf):
        q0 = pl.multiple_of(tq_tab[f] * tq, tq)
        k = k_ref[pl.ds(pl.multiple_of(tk_tab[f] * tk, tk), tk), :]
        for c, (mp, hf) in enumerate(chunks):
            qc = q_ref[pl.ds(q0 + hf * Q_CHUNK, Q_CHUNK), :]
            qc = jnp.where(in_map[mp], qc, jnp.zeros_like(qc))
            s = lax.dot_general(k, qc, _NT, preferred_element_type=F32)
            s_buf[c] = s
            mt_buf[:, c * Q_CHUNK:(c + 1) * Q_CHUNK] = jnp.max(s, axis=0, keepdims=True)

    def consume(f, s_buf, mt_buf, diagonal):
        j = tk_tab[f]
        m_all, acc_all = m_scr[...], acc_scr[...]
        m_out, acc_out = [], []
        for c, (mp, hf) in enumerate(chunks):
            cols = slice(c * Q_CHUNK, (c + 1) * Q_CHUNK)
            m_old = m_all[:, cols]
            if diagonal:
                rows = (hf + 1) * Q_CHUNK
                s = s_buf[c, :rows, :]
                k_pos = lax.broadcasted_iota(jnp.int32, s.shape, 0)
                q_pos = hf * Q_CHUNK + lax.broadcasted_iota(jnp.int32, s.shape, 1)
                s = jnp.where(k_pos <= q_pos, s, NEG_INF)
                m_new = jnp.maximum(m_old, jnp.max(s, axis=0, keepdims=True))
            else:
                rows = tk
                s = s_buf[c]
                m_new = jnp.maximum(m_old, mt_buf[:, cols])
            alpha = jnp.exp2(m_old - m_new)
            p = jnp.exp2((s - m_new).astype(BF16))
            pv = jnp.dot(vt_ref[j, :, :rows], p, preferred_element_type=F32)
            acc_out.append(alpha * acc_all[:, cols] + pv)
            m_out.append(m_new)
        if not diagonal:
            m_scr[...] = jnp.concatenate(m_out, axis=1)
            acc_scr[...] = jnp.concatenate(acc_out, axis=1)
            return
        acc = jnp.concatenate(acc_out, axis=1)
        o_all = acc[:D_HEAD] * (1.0 / acc[D_HEAD:D_HEAD + 1])
        o_t = o_all[:, :tq] - _lambda(lam_ref) * o_all[:, tq:]
        on = _rms_rows(o_t.T, go_ref[...]) * (1.0 - LAM_INIT)
        q_rows = pl.ds(pl.multiple_of(j * tq, tq), tq)
        u_ref[q_rows, :] = (on * _silu(g_ref[q_rows, :].astype(F32))).astype(BF16)
        m_scr[...] = jnp.full(m_scr.shape, NEG_INF, F32)
        acc_scr[...] = jnp.zeros_like(acc_scr)

    def step(f, cur, nxt):
        g = first_g + f
        seq, grp = lax.div(g, dec.groups_per_seq), lax.rem(g, dec.groups_per_seq)
        live = g < dec.n_groups

        @pl.when(g + 2 < dec.n_groups)
        def _():
            dec.start(g + 2)

        @pl.when(live)
        def _():
            dec.wait(g)

        @pl.when(live & (grp == 0))
        def _():
            dec.begin_sequence(seq)

        diagonal = tq_tab[f] == tk_tab[f]
        slot = lax.rem(g, DECODE_SLOTS)

        @pl.when(diagonal)
        def _():
            scores(f + 1, *nxt)
            consume(f, *cur, diagonal=True)
            dec.group(slot)

        @pl.when(jnp.logical_not(diagonal))
        def _():
            scores(f + 1, *nxt)
            consume(f, *cur, diagonal=False)
            dec.group(slot)

        @pl.when(live & (grp == dec.groups_per_seq - 1))
        def _():
            dec.end_sequence(seq)

    @pl.when(first_g == 0)
    def _():
        dec.start(0)
        dec.start(1)

    buf_a, buf_b = (sa_scr, mta_scr), (sb_scr, mtb_scr)
    scores(0, *buf_a)

    def pair(i, carry):
        step(2 * i, buf_a, buf_b)
        step(2 * i + 1, buf_b, buf_a)
        return carry

    assert n_steps % 2 == 0
    lax.fori_loop(0, n_steps // 2, pair, 0)


def _diff_attention(dq, dkb, dvt, dg, go, lam_rows, page_table, dqt_s, dq_s, dk_s, dv_s, dg_s, k_pages, v_pages):
    B, L, _ = dq.shape
    nk, tk = dvt.shape[1], dvt.shape[4]
    n_s, n_pages = page_table.shape
    assert L == nk * tk and tk % Q_CHUNK == 0 and n_pages % PAGES_PER_GROUP == 0
    pairs = [(qi, j) for qi in range(nk) for j in range(qi + 1)] + [(0, 0)]
    n_steps = len(pairs) - 1
    assert B * N_HEADS * n_steps >= n_s * (n_pages // PAGES_PER_GROUP) >= DECODE_SLOTS
    tq_tab = jnp.asarray([p[0] for p in pairs], jnp.int32)
    tk_tab = jnp.asarray([p[1] for p in pairs], jnp.int32)
    seq = pl.BlockSpec((None, L, D_HEAD), lambda b, h, *_: (b, 0, h))
    whole = lambda shape: pl.BlockSpec(shape, lambda b, h, *_: (0,) * len(shape))
    rows_s = whole((n_s, 1, D_GROUP))
    hbm = pl.BlockSpec(memory_space=pl.ANY)
    n_maps = 2 * N_HEADS
    group_buf = pltpu.VMEM((DECODE_SLOTS, PAGES_PER_GROUP, D_GROUP, LANES), F32)
    grid_spec = pltpu.PrefetchScalarGridSpec(
        num_scalar_prefetch=3,
        grid=(B, N_HEADS),
        in_specs=[seq, seq,
                  pl.BlockSpec((None, nk, None, VT_ROWS, tk), lambda b, h, *_: (b, 0, h, 0, 0)),
                  seq, whole((1, D_HEAD)), whole((4, D_MAP)),
                  whole((D_GROUP, LANES)), rows_s, rows_s, rows_s, rows_s, hbm, hbm],
        out_specs=[seq, rows_s],
        scratch_shapes=[pltpu.VMEM((VT_ROWS, 2 * tk), F32), pltpu.VMEM((1, 2 * tk), F32)]
                       + [pltpu.VMEM((2 * tk // Q_CHUNK, tk, Q_CHUNK), F32), pltpu.VMEM((1, 2 * tk), F32)] * 2
                       + [group_buf, group_buf, pltpu.SemaphoreType.DMA((DECODE_SLOTS,)),
                          pltpu.VMEM((D_GROUP, LANES), F32), pltpu.VMEM((n_maps, D_GROUP), F32),
                          pltpu.VMEM((n_maps, 1), F32), pltpu.VMEM((n_maps, 1), F32)],
    )
    return pl.pallas_call(
        _diff_prompt_kernel,
        grid_spec=grid_spec,
        out_shape=[jax.ShapeDtypeStruct((B, L, D_GROUP), BF16), jax.ShapeDtypeStruct((n_s, 1, D_GROUP), F32)],
        compiler_params=pltpu.CompilerParams(
            dimension_semantics=("arbitrary", "arbitrary"), vmem_limit_bytes=VMEM_LIMIT_BYTES),
        name="diff_attention",
    )(tq_tab, tk_tab, page_table, dq, dkb, dvt, dg, go, lam_rows, dqt_s, dq_s, dk_s, dv_s, dg_s, k_pages, v_pages)


def _out_proj_kernel(x_ref, ur_ref, ud_ref, w_ref, y_ref):
    y_ref[...] = (x_ref[...]
                  + jnp.dot(ur_ref[...].astype(BF16), w_ref[:D_GROUP, :], preferred_element_type=F32)
                  + jnp.dot(ud_ref[...].astype(BF16), w_ref[D_GROUP:, :], preferred_element_type=F32))


def _out_proj(x2d, ur, ud, w_bf, tm):
    T = x2d.shape[0]
    tok = lambda width: pl.BlockSpec((tm, width), lambda i: (i, 0))
    return pl.pallas_call(
        _out_proj_kernel,
        grid=(T // tm,),
        in_specs=[tok(D_MODEL), tok(D_GROUP), tok(D_GROUP),
                  pl.BlockSpec((2 * D_GROUP, D_MODEL), lambda i: (0, 0))],
        out_specs=tok(D_MODEL),
        out_shape=jax.ShapeDtypeStruct((T, D_MODEL), F32),
        compiler_params=pltpu.CompilerParams(
            dimension_semantics=("arbitrary",), vmem_limit_bytes=VMEM_LIMIT_BYTES),
        name="out_proj",
    )(x2d, ur, ud, w_bf)


def _proj_sample_kernel(x_ref, ng_ref, w_ref, rc_ref, rs_ref, dc_ref, da_ref, db_ref, gq_ref, gk_ref, seg_ref,
                        rqt_ref, rkt_ref, dqt_ref, rv_ref, rg_ref, dq_ref, dk_ref, dv_ref, dg_ref):
    hb = _normed_input(x_ref, ng_ref)
    n = hb.shape[0]
    rc, rs = rc_ref[...], rs_ref[...]
    dc, da, db = dc_ref[...], da_ref[...], db_ref[...]
    seg = seg_ref[...]
    pad = jnp.zeros((LANES - n, D_HEAD), F32)

    def transposed(xh):
        return jnp.concatenate([xh, pad], axis=0).T

    z = _z_group(hb, w_ref, 0)
    for h in range(N_HEADS):
        rqt_ref[_head(h), :] = transposed(_rope_full(z[:, _head(h)], rc, rs))
    z = _z_group(hb, w_ref, 1)
    rck, rsk = rc * RET_K_SCALE, rs * RET_K_SCALE
    for h in range(N_HEADS):
        rkt_ref[_head(h), :] = transposed(_rope_full(z[:, _head(h)], rck, rsk))
    rv_ref[...] = _z_group(hb, w_ref, 2)
    rg_ref[...] = _z_group(hb, w_ref, 3)
    z = _z_group(hb, w_ref, 4)
    dcq, daq, dbq = dc * DIFF_Q_SCALE, da * DIFF_Q_SCALE, db * DIFF_Q_SCALE
    for h in range(N_HEADS):
        qh = _rope_partial(_map_rms(z[:, _head(h)], seg, gq_ref[...]), dcq, daq, dbq)
        dq_ref[:, _head(h)] = qh
        dqt_ref[_head(h), :] = transposed(qh)
    z = _z_group(hb, w_ref, 5)
    for h in range(N_HEADS):
        dk_ref[:, _head(h)] = _rope_partial(_map_rms(z[:, _head(h)], seg, gk_ref[...]), dc, da, db)
    dv_ref[...] = _z_group(hb, w_ref, 6)
    dg_ref[...] = _z_group(hb, w_ref, 7)


def _proj_sample(x, ng, w_bf, tabs, gq, gk, seg):
    n = x.shape[0]
    full = lambda shape: pl.BlockSpec(shape, lambda i: (0,) * len(shape))
    row_out = jax.ShapeDtypeStruct((n, D_GROUP), F32)
    col_out = jax.ShapeDtypeStruct((D_GROUP, LANES), F32)
    return pl.pallas_call(
        _proj_sample_kernel,
        grid=(1,),
        in_specs=[full((n, D_MODEL)), full((1, D_MODEL)), full((D_MODEL, 8 * D_GROUP))]
                 + [full((n, LANES))] * 5 + [full((1, LANES)), full((1, LANES)), full((LANES, LANES))],
        out_specs=[full((D_GROUP, LANES))] * 3 + [full((n, D_GROUP))] * 6,
        out_shape=[col_out] * 3 + [row_out] * 6,
        compiler_params=pltpu.CompilerParams(
            dimension_semantics=("arbitrary",), vmem_limit_bytes=VMEM_LIMIT_BYTES),
        name="proj_sample",
    )(x, ng, w_bf, *tabs, gq, gk, seg)


def _ret_sample_kernel(qt_ref, kt_ref, v_ref, g_ref, s_ref, go_ref, u_ref, snew_ref):
    per_step = v_ref.shape[0]
    lane = lax.broadcasted_iota(jnp.int32, (D_HEAD, LANES), 1)

    def column(ref, h, b):
        return jnp.sum(jnp.where(lane == b, ref[_head(h), :], 0.0), axis=1, keepdims=True)

    for i in range(per_step):
        b = pl.program_id(0) * per_step + i
        for h in range(N_HEADS):
            gamma = math.exp(LOG_GAMMA[h])
            q_col, k_col = column(qt_ref, h, b), column(kt_ref, h, b)
            v_row = v_ref[i, :, _head(h)]
            s_old = s_ref[i, h]
            inner = jnp.sum(q_col * k_col, axis=0, keepdims=True)
            o = inner * v_row + jnp.sum((q_col * gamma) * s_old, axis=0, keepdims=True)
            snew_ref[i, h] = gamma * s_old + k_col * v_row
            u_ref[i, :, _head(h)] = _rms_rows(o, go_ref[...]) * _silu(g_ref[i, :, _head(h)])


def _ret_sample(rqt, rkt, rv3, rg3, state, go):
    n = state.shape[0]
    per_step = RET_SAMPLE_SEQS
    assert n % per_step == 0
    col = pl.BlockSpec((D_GROUP, LANES), lambda b: (0, 0))
    row = pl.BlockSpec((per_step, 1, D_GROUP), lambda b: (b, 0, 0))
    st = pl.BlockSpec((per_step, N_HEADS, D_HEAD, D_HEAD), lambda b: (b, 0, 0, 0))
    return pl.pallas_call(
        _ret_sample_kernel,
        grid=(n // per_step,),
        in_specs=[col, col, row, row, st, pl.BlockSpec((1, D_HEAD), lambda b: (0, 0))],
        out_specs=[row, st],
        out_shape=[jax.ShapeDtypeStruct((n, 1, D_GROUP), F32),
                   jax.ShapeDtypeStruct((n, N_HEADS, D_HEAD, D_HEAD), F32)],
        compiler_params=pltpu.CompilerParams(dimension_semantics=("arbitrary",)),
        name="ret_sample",
    )(rqt, rkt, rv3, rg3, state, go)


def _rope_tables(pos):
    posf = np.asarray(pos, np.float64)[:, None]
    fr = RET_ROPE_THETA ** (-np.arange(0, D_HEAD, 2, dtype=np.float64) / D_HEAD)
    ang = posf * fr[None, :]
    cos, sin = np.cos(ang), np.sin(ang)
    rc = np.concatenate([cos, cos], axis=-1)
    rs = np.concatenate([-sin, sin], axis=-1)
    half = N_ROT_DIFF // 2
    fd = ROPE_THETA ** (-np.arange(0, N_ROT_DIFF, 2, dtype=np.float64) / N_ROT_DIFF)
    angd = posf * fd[None, :]
    cd, sd = np.cos(angd), np.sin(angd)
    n = posf.shape[0]
    ones = np.ones((n, D_MAP - N_ROT_DIFF))
    zeros = np.zeros((n, D_MAP - N_ROT_DIFF))
    z8 = np.zeros((n, half))
    dc = np.concatenate([cd, cd, ones], axis=-1)
    da = np.concatenate([-sd, z8, zeros], axis=-1)
    db = np.concatenate([z8, sd, zeros], axis=-1)
    twice = lambda t: np.concatenate([t, t], axis=-1)
    return tuple(jnp.asarray(t, F32) for t in (rc, rs, twice(dc), twice(da), twice(db)))


def kernel(x_prompt, x_sample, cache_k, cache_v, state_ret, page_table, norm_g, w_in, w_out,
           q_norm_g, k_norm_g, ret_out_g, diff_out_g, lambda_q1, lambda_k1, lambda_q2, lambda_k2):
    B, L, _ = x_prompt.shape
    n_s = x_sample.shape[0]
    n_pages = page_table.shape[1]
    past = n_pages * PAGE_SIZE
    l = 0

    ng = norm_g[l][None, :]
    w_in_bf = w_in[l].astype(BF16)
    w_out_bf = w_out[l].astype(BF16)
    gq = jnp.tile(q_norm_g[l], 2)[None, :]
    gk = jnp.tile(k_norm_g[l], 2)[None, :]
    go_ret = ret_out_g[l][None, :]
    go_diff = diff_out_g[l][None, :]
    lam_rows = jnp.stack([lambda_q1[l], lambda_k1[l], lambda_q2[l], lambda_k2[l]])
    lane = np.arange(LANES)
    seg = jnp.asarray(np.where((lane[:, None] // D_MAP) == (lane[None, :] // D_MAP), 1.0 / D_MAP, 0.0), BF16)

    tabs_p = _rope_tables(np.arange(L))
    rq, rk, rv, rg, dq, dk, dkb, dv, dvt, dg = _proj_prompt(x_prompt, ng, w_in_bf, tabs_p, gq, gk, seg)
    tabs_s = _rope_tables(np.full((n_s,), past))
    rqt, rkt, dqt, rv_s, rg_s, dq_s, dk_s, dv_s, dg_s = _proj_sample(
        x_sample.reshape(n_s, D_MODEL), ng, w_in_bf, tabs_s, gq, gk, seg)
    as_rows = lambda t: t.reshape(n_s, 1, D_GROUP)

    u_ret, s_fin = _ret_prompt(rq, rk, rv, rg, go_ret)
    u_ret_s, s_new = _ret_sample(rqt, rkt, as_rows(rv_s), as_rows(rg_s), state_ret[l], go_ret)

    n_phys = cache_k.shape[1]
    k_pages = jnp.transpose(cache_k[l], (0, 2, 3, 4, 1)).reshape(n_phys, D_GROUP, PAGE_SIZE)
    v_pages = cache_v[l].reshape(n_phys, PAGE_SIZE * N_HEADS, D_HEAD)
    u_diff, u_diff_s = _diff_attention(dq, dkb, dvt, dg, go_diff, lam_rows, page_table, dqt, as_rows(dq_s),
                                       as_rows(dk_s), as_rows(dv_s), as_rows(dg_s), k_pages, v_pages)

    y_prompt = _out_proj(x_prompt.reshape(B * L, D_MODEL), u_ret.reshape(B * L, D_GROUP),
                         u_diff.reshape(B * L, D_GROUP), w_out_bf, OUT_TILE).reshape(B, L, D_MODEL)
    y_sample = _out_proj(x_sample.reshape(n_s, D_MODEL), u_ret_s.reshape(n_s, D_GROUP),
                         u_diff_s.reshape(n_s, D_GROUP), w_out_bf, n_s).reshape(n_s, 1, D_MODEL)

    return (y_prompt, y_sample,
            dk.reshape(1, B, L, N_HEADS, 2, D_MAP), dv.reshape(1, B, L, N_HEADS, D_HEAD), s_fin[None],
            dk_s.reshape(1, n_s, 1, N_HEADS, 2, D_MAP), dv_s.reshape(1, n_s, 1, N_HEADS, D_HEAD), s_new[None])
```
